```python
import jax, jax.numpy as jnp
from jax import lax
import numpy as np

D_MODEL = 4096
BATCH = 4
SEQ = 2048
DEPTH = 1

MIX_W = D_MODEL
RWKV_W = MIX_W // 2
RWKV_HEAD = 64
N_RWKV_HEADS = RWKV_W // RWKV_HEAD
DECAY_LORA = 96
AAA_LORA = 96
GATE_LORA = 256
LRU_W = MIX_W - RWKV_W
LRU_BLOCK_W = 128
LRU_BLOCKS = LRU_W // LRU_BLOCK_W
CONV_WIDTH = 4
LRU_C = 8.0
D_FF = ((8 * D_MODEL + 3 * 256 - 1) // (3 * 256)) * 256
RWKV_COLS = 3 * RWKV_W + DECAY_LORA + AAA_LORA + GATE_LORA
IN_COLS = RWKV_COLS + 2 * LRU_W
NORM_EPS = 1e-6
GN_EPS = 64e-5

kernel_name = "hymba_rwkv7_rglru_swiglu"


def _rmsnorm(x, g):
    xf = x.astype(jnp.float32)
    return xf * lax.rsqrt(jnp.mean(xf * xf, axis=-1, keepdims=True) + NORM_EPS) * g.astype(jnp.float32)


def _token_shift(p):
    return jnp.pad(p, ((0, 0), (1, 0), (0, 0)))[:, :-1]


def _rwkv7_scan(r, w, k, v, a, b):
    Bsz, T, H, N = r.shape

    def step(S, inp):
        r_t, w_t, k_t, v_t, a_t, b_t = inp
        sa = jnp.einsum('bhvk,bhk->bhv', S, a_t)
        S = (S * w_t[:, :, None, :] + sa[..., None] * b_t[:, :, None, :]
             + v_t[..., None] * k_t[:, :, None, :])
        y_t = jnp.einsum('bhvk,bhk->bhv', S, r_t)
        return S, y_t

    seq = tuple(jnp.swapaxes(t, 0, 1) for t in (r, w, k, v, a, b))
    S0 = jnp.zeros((Bsz, H, N, N), jnp.float32)
    _, ys = lax.scan(step, S0, seq)
    return jnp.swapaxes(ys, 0, 1)


def _rwkv7_mixer(p, mu, w0, w2, a0, a2, g2, k_k, k_a, r_k, ln_g, ln_b):
    Bsz, T, _ = p.shape
    p = p + (_token_shift(p) - p) * mu
    o1, o2, o3 = RWKV_W, 2 * RWKV_W, 3 * RWKV_W
    r, k, v, wd, ad, gd = jnp.split(p, [o1, o2, o3, o3 + DECAY_LORA, o3 + DECAY_LORA + AAA_LORA], axis=-1)
    w = -jax.nn.softplus(-(w0 + jnp.tanh(wd) @ w2)) - 0.5
    decay = jnp.exp(-jnp.exp(w))
    a = jax.nn.sigmoid(a0 + ad @ a2)
    g = jax.nn.sigmoid(gd) @ g2

    def heads(t):
        return t.reshape(Bsz, T, N_RWKV_HEADS, RWKV_HEAD)

    kk = heads(k * k_k)
    kk = kk * lax.rsqrt(jnp.maximum(jnp.sum(kk * kk, axis=-1, keepdims=True), 1e-24))
    k = k * (1.0 + (a - 1.0) * k_a)
    r, k, v, decay, a = heads(r), heads(k), heads(v), heads(decay), heads(a)
    y = _rwkv7_scan(r, decay, k, v, -kk, kk * a)
    mean = jnp.mean(y, axis=-1, keepdims=True)
    var = jnp.mean(jnp.square(y - mean), axis=-1, keepdims=True)
    y = (y - mean) * lax.rsqrt(var + GN_EPS)
    y = y * ln_g.reshape(N_RWKV_HEADS, RWKV_HEAD) + ln_b.reshape(N_RWKV_HEADS, RWKV_HEAD)
    y = y + jnp.sum(r * k * r_k, axis=-1, keepdims=True) * v
    return y.reshape(Bsz, T, RWKV_W) * g


def _rglru_mixer(p, conv_w, conv_b, wr, br, wi, bi, lam, norm_g):
    Bsz, T, _ = p.shape
    xb, gate = jnp.split(p, [LRU_W], axis=-1)
    xpad = jnp.pad(xb, ((0, 0), (CONV_WIDTH - 1, 0), (0, 0)))
    xc = conv_b + xpad[:, 0:T] * conv_w[0]
    for j in range(1, CONV_WIDTH):
        xc = xc + xpad[:, j:j + T] * conv_w[j]
    xh = xc.reshape(Bsz, T, LRU_BLOCKS, LRU_BLOCK_W)
    rg = jax.nn.sigmoid(jnp.einsum('bthi,hij->bthj', xh, wr).reshape(Bsz, T, LRU_W) + br)
    ig = jax.nn.sigmoid(jnp.einsum('bthi,hij->bthj', xh, wi).reshape(Bsz, T, LRU_W) + bi)
    log_a = -LRU_C * rg * jax.nn.softplus(-lam)
    a = jnp.exp(log_a)
    mult = jnp.sqrt(-jnp.expm1(2.0 * log_a))
    first = (jnp.arange(T) == 0)[None, :, None]
    mult = jnp.where(first, 1.0, mult)
    bx = mult * ig * xc

    def combine(left, right):
        a1, b1 = left
        a2, b2 = right
        return a1 * a2, a2 * b1 + b2

    _, h = lax.associative_scan(combine, (a, bx), axis=1)
    y = h * jax.nn.gelu(gate)
    return _rmsnorm(y, norm_g)


def setup_inputs(seed: int = 0) -> dict:
    key = jax.random.key(seed)
    ks = jax.random.split(key, 32)
    f32 = jnp.float32
    L = DEPTH

    def nrm(k, shape, scale):
        return jax.random.normal(k, shape, f32) * scale

    u = jax.random.uniform(ks[20], (L, LRU_W), f32, 0.9, 0.999)
    a_base = u ** (1.0 / LRU_C)
    lru_lambda = jnp.log(a_base) - jnp.log1p(-a_base)
    return {
        "x": nrm(ks[0], (BATCH, SEQ, D_MODEL), 1.0),
        "norm_mix_g": 1.0 + nrm(ks[1], (L, D_MODEL), 0.02),
        "w_in": nrm(ks[2], (L, D_MODEL, IN_COLS), D_MODEL ** -0.5),
        "mu_shift": jax.random.uniform(ks[3], (L, RWKV_COLS), f32),
        "rwkv_w0": jax.random.uniform(ks[4], (L, RWKV_W), f32, -6.0, 0.0),
        "rwkv_w2": nrm(ks[5], (L, DECAY_LORA, RWKV_W), 0.1 * DECAY_LORA ** -0.5),
        "rwkv_a0": nrm(ks[6], (L, RWKV_W), 0.1),
        "rwkv_a2": nrm(ks[7], (L, AAA_LORA, RWKV_W), AAA_LORA ** -0.5),
        "rwkv_g2": nrm(ks[8], (L, GATE_LORA, RWKV_W), GATE_LORA ** -0.5),
        "rwkv_k_k": 0.85 + nrm(ks[9], (L, RWKV_W), 0.02),
        "rwkv_k_a": 1.0 + nrm(ks[10], (L, RWKV_W), 0.02),
        "rwkv_r_k": nrm(ks[11], (L, N_RWKV_HEADS, RWKV_HEAD), 0.1),
        "rwkv_ln_g": 1.0 + nrm(ks[12], (L, RWKV_W), 0.02),
        "rwkv_ln_b": nrm(ks[13], (L, RWKV_W), 0.01),
        "conv_w": nrm(ks[14], (L, CONV_WIDTH, LRU_W), CONV_WIDTH ** -0.5),
        "conv_b": nrm(ks[15], (L, LRU_W), 0.01),
        "lru_wr": nrm(ks[16], (L, LRU_BLOCKS, LRU_BLOCK_W, LRU_BLOCK_W), LRU_BLOCK_W ** -0.5),
        "lru_br": nrm(ks[17], (L, LRU_W), 0.01),
        "lru_wi": nrm(ks[18], (L, LRU_BLOCKS, LRU_BLOCK_W, LRU_BLOCK_W), LRU_BLOCK_W ** -0.5),
        "lru_bi": nrm(ks[19], (L, LRU_W), 0.01),
        "lru_lambda": lru_lambda,
        "lru_norm_g": 1.0 + nrm(ks[21], (L, LRU_W), 0.02),
        "w_out": nrm(ks[22], (L, MIX_W, D_MODEL), MIX_W ** -0.5),
        "norm_ffn_g": 1.0 + nrm(ks[23], (L, D_MODEL), 0.02),
        "ffn_w_gate": nrm(ks[24], (L, D_MODEL, D_FF), D_MODEL ** -0.5),
        "ffn_w_up": nrm(ks[25], (L, D_MODEL, D_FF), D_MODEL ** -0.5),
        "ffn_w_down": nrm(ks[26], (L, D_FF, D_MODEL), D_FF ** -0.5),
        "norm_final_g": 1.0 + nrm(ks[27], (D_MODEL,), 0.02),
    }


def reference(x, norm_mix_g, w_in, mu_shift, rwkv_w0, rwkv_w2, rwkv_a0, rwkv_a2, rwkv_g2,
              rwkv_k_k, rwkv_k_a, rwkv_r_k, rwkv_ln_g, rwkv_ln_b, conv_w, conv_b,
              lru_wr, lru_br, lru_wi, lru_bi, lru_lambda, lru_norm_g, w_out,
              norm_ffn_g, ffn_w_gate, ffn_w_up, ffn_w_down, norm_final_g):
    h = x.astype(jnp.float32)
    for l in range(DEPTH):
        u = _rmsnorm(h, norm_mix_g[l])
        p = u @ w_in[l]
        y_a = _rwkv7_mixer(p[..., :RWKV_COLS], mu_shift[l], rwkv_w0[l], rwkv_w2[l], rwkv_a0[l],
                           rwkv_a2[l], rwkv_g2[l], rwkv_k_k[l], rwkv_k_a[l], rwkv_r_k[l],
                           rwkv_ln_g[l], rwkv_ln_b[l])
        y_b = _rglru_mixer(p[..., RWKV_COLS:], conv_w[l], conv_b[l], lru_wr[l], lru_br[l],
                           lru_wi[l], lru_bi[l], lru_lambda[l], lru_norm_g[l])
        h = h + jnp.concatenate([y_a, y_b], axis=-1) @ w_out[l]
        u = _rmsnorm(h, norm_ffn_g[l])
        h = h + (jax.nn.silu(u @ ffn_w_gate[l]) * (u @ ffn_w_up[l])) @ ffn_w_down[l]
    return _rmsnorm(h, norm_final_g).astype(x.dtype)
```

```python
import functools

import jax
import jax.numpy as jnp
from jax import lax
from jax.experimental import pallas as pl
from jax.experimental.pallas import tpu as pltpu

F32 = jnp.float32
BF16 = jnp.bfloat16

D_MODEL = 4096
RWKV_W = 2048
HEAD = 64
DECAY_LORA = 96
AAA_LORA = 96
GATE_LORA = 256
LRU_W = 2048
LRU_BLOCK = 128
LRU_BLOCKS = LRU_W // LRU_BLOCK
CONV_WIDTH = 4
LRU_C = 8.0
NORM_EPS = 1e-6
GN_EPS = 64e-5

LANES = 128
SUBLANES = 8
LORA_PAD = 128
LORA_W = 2 * LORA_PAD + GATE_LORA
OFF_LRU_X = 0
OFF_LRU_G = LRU_W
OFF_R = 2 * LRU_W
OFF_K = OFF_R + RWKV_W
OFF_V = OFF_K + RWKV_W
OFF_LORA = OFF_V + RWKV_W
P_COLS = OFF_LORA + LORA_W

VMEM_LIMIT = 56 * 1024 * 1024

CHUNK = 64
CHUNK_SHIFT = CHUNK.bit_length() - 1
PAIR_W = 2 * HEAD


def _cparams(sem):
    return pltpu.CompilerParams(dimension_semantics=sem, vmem_limit_bytes=VMEM_LIMIT)


def _dot(a, b):
    return jnp.dot(a, b, preferred_element_type=F32)


def _dot_nt(a, b):
    return lax.dot_general(a, b, (((1,), (1,)), ((), ())), preferred_element_type=F32)


def _split(x):
    hi = x.astype(BF16)
    lo = (x - hi.astype(F32)).astype(BF16)
    return hi, lo


def _dot_x3(a, b):
    ah, al = _split(a)
    bh, bl = _split(b)
    return _dot(ah, bh) + _dot(al, bh) + _dot(ah, bl)


def _dot_x2_exact_rhs(a, b_bf16):
    ah, al = _split(a)
    return _dot(ah, b_bf16) + _dot(al, b_bf16)


def _softplus(x):
    return jnp.maximum(x, 0.0) + jnp.log1p(jnp.exp(-jnp.abs(x)))


def _shift_rows(x, prev_tail, s):
    rolled = pltpu.roll(x, s, 0)
    fix = pltpu.roll(prev_tail, s, 0)
    row = lax.broadcasted_iota(jnp.int32, (SUBLANES, x.shape[1]), 0)
    head = jnp.where(row < s, fix, rolled[:SUBLANES])
    return jnp.concatenate([head, rolled[SUBLANES:]], axis=0)


def _norm_matmul_kernel(x_ref, g_ref, w_ref, o_ref, u_ref):
    @pl.when(pl.program_id(1) == 0)
    def _():
        x = x_ref[...]
        ms = jnp.mean(x * x, axis=-1, keepdims=True)
        u_ref[...] = (x * lax.rsqrt(ms + NORM_EPS) * g_ref[...]).astype(BF16)

    o_ref[...] = _dot(u_ref[...], w_ref[...]).astype(o_ref.dtype)


def _norm_matmul(x2d, g, w_bf16, tm, tn):
    m, d = x2d.shape
    n = w_bf16.shape[1]
    return pl.pallas_call(
        _norm_matmul_kernel,
        grid=(m // tm, n // tn),
        in_specs=[
            pl.BlockSpec((tm, d), lambda i, j: (i, 0)),
            pl.BlockSpec((1, d), lambda i, j: (0, 0)),
            pl.BlockSpec((d, tn), lambda i, j: (0, j)),
        ],
        out_specs=pl.BlockSpec((tm, tn), lambda i, j: (i, j)),
        out_shape=jax.ShapeDtypeStruct((m, n), F32),
        scratch_shapes=[pltpu.VMEM((tm, d), BF16)],
        compiler_params=_cparams(("arbitrary", "arbitrary")),
        name="norm_inproj",
    )(x2d, g.reshape(1, d), w_bf16)


def _rwkv_kernel(r_ref, k_ref, v_ref, lo_ref,
                 mur_ref, muk_ref, muv_ref, mulo_ref,
                 w0_ref, a0_ref, kk_ref, ka_ref, rk_ref, lng_ref, lnb_ref,
                 w2_ref, a2_ref, g2_ref,
                 y_ref,
                 tail_ref, lotail_ref, loact_ref, s_ref,
                 ats_ref, rts_ref, arb_ref, tm_ref, bls_ref,
                 x0_ref, y0_ref, g0_ref, el_ref, yraw_ref,
                 *, tt, pairs):
    t = pl.program_id(1)
    g = pl.program_id(2)
    nc = tt // CHUNK
    first = t == 0

    @pl.when(first)
    def _():
        for i in range(pairs * 3):
            tail_ref[g * pairs * 3 + i] = jnp.zeros((SUBLANES, PAIR_W), F32)

    @pl.when(first & (g == 0))
    def _():
        lotail_ref[...] = jnp.zeros_like(lotail_ref)

    @pl.when(g == 0)
    def _():
        lo = lo_ref[0]
        lo_prev = _shift_rows(lo, lotail_ref[...], 1)
        lotail_ref[...] = lo[tt - SUBLANES:]
        xs = lo + (lo_prev - lo) * mulo_ref[...]
        loact_ref[:, 0:LORA_PAD] = jnp.tanh(xs[:, 0:LORA_PAD])
        loact_ref[:, LORA_PAD:2 * LORA_PAD] = xs[:, LORA_PAD:2 * LORA_PAD]
        loact_ref[:, 2 * LORA_PAD:] = jax.nn.sigmoid(xs[:, 2 * LORA_PAD:])

    lane = lax.broadcasted_iota(jnp.int32, (1, PAIR_W), 1)
    m0 = lane < HEAD
    row = lax.broadcasted_iota(jnp.int32, (PAIR_W, PAIR_W), 0)
    col = lax.broadcasted_iota(jnp.int32, (PAIR_W, PAIR_W), 1)
    same = (row >> CHUNK_SHIFT) == (col >> CHUNK_SHIFT)
    strict = same & (col < row)
    incl = same & (col <= row)
    eye = (row == col).astype(F32)
    seg_ones = same.astype(BF16)
    tr = lax.broadcasted_iota(jnp.int32, (CHUNK, CHUNK), 0)
    tc = lax.broadcasted_iota(jnp.int32, (CHUNK, CHUNK), 1)
    tri = (tc <= tr).astype(BF16)

    def stack(xc):
        return jnp.concatenate([jnp.where(m0, xc, 0.0), jnp.where(m0, 0.0, xc)], axis=0)

    def seg_sum(x):
        return _dot_x2_exact_rhs(x, seg_ones)

    wdt = loact_ref[:, 0:LORA_PAD]
    adx = loact_ref[:, LORA_PAD:2 * LORA_PAD]
    gds = loact_ref[:, 2 * LORA_PAD:]

    for p in range(pairs):
        ls = slice(p * PAIR_W, (p + 1) * PAIR_W)
        slot = g * pairs + p

        def shifted(ref, mu_ref, idx):
            x = ref[0, :, ls]
            xp = _shift_rows(x, tail_ref[slot * 3 + idx], 1)
            tail_ref[slot * 3 + idx] = x[tt - SUBLANES:]
            return x + (xp - x) * mu_ref[:, ls]

        r = shifted(r_ref, mur_ref, 0)
        k = shifted(k_ref, muk_ref, 1)
        v = shifted(v_ref, muv_ref, 2)

        wlog = -_softplus(-(w0_ref[:, ls] + _dot_x3(wdt, w2_ref[:, ls]))) - 0.5
        lw = -jnp.exp(wlog)
        asig = jax.nn.sigmoid(a0_ref[:, ls] + _dot_x3(adx, a2_ref[:, ls]))
        gate = _dot_x3(gds, g2_ref[:, ls])

        kk = k * kk_ref[:, ls]
        kk = kk * lax.rsqrt(jnp.maximum(seg_sum(kk * kk), 1e-24))
        k = k * (1.0 + (asig - 1.0) * ka_ref[:, ls])
        a_in = -kk
        b_in = kk * asig

        for c in range(nc):
            cs = slice(c * CHUNK, (c + 1) * CHUNK)
            lwc = lw[cs]
            cum = _dot_x2_exact_rhs_lhs(tri, lwc)
            cl = cum[CHUNK - 1:CHUNK]
            e_c = jnp.exp(cum)
            e_p = jnp.exp(cum - lwc)
            e_n = jnp.exp(-cum)
            e_l = jnp.exp(cl - cum)
            at = stack(a_in[cs] * e_p).astype(BF16)
            rt = stack(r[cs] * e_c).astype(BF16)
            bt = stack(b_in[cs] * e_n).astype(BF16)
            kt = stack(k[cs] * e_n).astype(BF16)
            btl = stack(b_in[cs] * e_l).astype(BF16)
            ktl = stack(k[cs] * e_l).astype(BF16)
            vs_f = stack(v[cs])
            vs = vs_f.astype(BF16)
            aab = jnp.where(strict, _dot_nt(at, bt), 0.0)
            aak = jnp.where(strict, _dot_nt(at, kt), 0.0)
            arb = jnp.where(incl, _dot_nt(rt, bt), 0.0)
            ark = jnp.where(incl, _dot_nt(rt, kt), 0.0)
            npow = aab.astype(BF16)
            tmat = eye + aab
            levels = CHUNK.bit_length() - 2
            for _ in range(levels):
                n2 = _dot(npow, npow)
                npow = n2.astype(BF16)
                tmat = tmat + _dot(tmat.astype(BF16), npow)
            ats_ref[p, c] = at
            rts_ref[p, c] = rt
            arb_ref[p, c] = arb.astype(BF16)
            tm_ref[p, c] = tmat.astype(BF16)
            bls_ref[p, c] = btl
            x0_ref[p, c] = _dot(aak.astype(BF16), vs)
            y0_ref[p, c] = _dot(ark.astype(BF16), vs)
            g0_ref[p, c] = _dot(vs_f.T.astype(BF16), ktl)
            el_ref[p, c] = jnp.broadcast_to(jnp.exp(cl), (SUBLANES, PAIR_W))

        yraw_ref[p, 1] = r * k * rk_ref[:, ls]
        yraw_ref[p, 2] = v
        yraw_ref[p, 3] = gate

    @pl.when(first)
    def _():
        for p in range(pairs):
            s_ref[g * pairs + p] = jnp.zeros((PAIR_W, PAIR_W), F32)

    def chunk_step(c, carry):
        for p in range(pairs):
            slot = g * pairs + p
            s = s_ref[slot]
            sb = s.astype(BF16)
            x = _dot_nt(ats_ref[p, c], sb) + x0_ref[p, c]
            u = _dot(tm_ref[p, c], x.astype(BF16))
            y2 = _dot_nt(rts_ref[p, c], sb) + _dot(arb_ref[p, c], u.astype(BF16)) + y0_ref[p, c]
            off = pl.multiple_of(c * CHUNK, CHUNK)
            yraw_ref[p, 0, pl.ds(off, CHUNK), :] = y2[:CHUNK] + y2[CHUNK:]
            s_ref[slot] = (s * el_ref[p, c][0:1] + _dot(u.T.astype(BF16), bls_ref[p, c]) + g0_ref[p, c])
        return carry

    lax.fori_loop(0, nc, chunk_step, 0)

    for p in range(pairs):
        ls = slice(p * PAIR_W, (p + 1) * PAIR_W)
        y = yraw_ref[p, 0]
        mean = seg_sum(y) * (1.0 / HEAD)
        yc = y - mean
        var = seg_sum(yc * yc) * (1.0 / HEAD)
        yn = yc * lax.rsqrt(var + GN_EPS) * lng_ref[:, ls] + lnb_ref[:, ls]
        bonus = seg_sum(yraw_ref[p, 1])
        out = (yn + bonus * yraw_ref[p, 2]) * yraw_ref[p, 3]
        y_ref[0, :, ls] = out.astype(y_ref.dtype)


def _dot_x2_exact_rhs_lhs(lhs_bf16, x):
    xh, xl = _split(x)
    return _dot(lhs_bf16, xh) + _dot(lhs_bf16, xl)


def _rwkv_mixer(p3, mu, w0, w2p, a0, a2p, g2, k_k, k_a, r_k, ln_g, ln_b, tt, pairs):
    b, t, _ = p3.shape
    gw = pairs * PAIR_W
    ng = RWKV_W // gw
    nc = tt // CHUNK
    row = lambda a: a.reshape(1, -1)
    col_spec = lambda off: pl.BlockSpec((1, gw), lambda bi, ti, gi, off=off: (0, off // gw + gi))
    act_spec = lambda off: pl.BlockSpec((1, tt, gw), lambda bi, ti, gi, off=off: (bi, ti, off // gw + gi))
    kern = functools.partial(_rwkv_kernel, tt=tt, pairs=pairs)
    pair_buf = lambda dt: pltpu.VMEM((pairs, nc, PAIR_W, PAIR_W), dt)
    return pl.pallas_call(
        kern,
        grid=(b, t // tt, ng),
        in_specs=[
            act_spec(OFF_R), act_spec(OFF_K), act_spec(OFF_V),
            pl.BlockSpec((1, tt, LORA_W), lambda bi, ti, gi: (bi, ti, OFF_LORA // LORA_W)),
            col_spec(OFF_R), col_spec(OFF_K), col_spec(OFF_V),
            pl.BlockSpec((1, LORA_W), lambda bi, ti, gi: (0, OFF_LORA // LORA_W)),
            col_spec(0), col_spec(0), col_spec(0), col_spec(0), col_spec(0), col_spec(0), col_spec(0),
            pl.BlockSpec((LORA_PAD, gw), lambda bi, ti, gi: (0, gi)),
            pl.BlockSpec((LORA_PAD, gw), lambda bi, ti, gi: (0, gi)),
            pl.BlockSpec((GATE_LORA, gw), lambda bi, ti, gi: (0, gi)),
        ],
        out_specs=pl.BlockSpec((1, tt, gw), lambda bi, ti, gi: (bi, ti, gi)),
        out_shape=jax.ShapeDtypeStruct((b, t, RWKV_W), BF16),
        scratch_shapes=[
            pltpu.VMEM((ng * pairs * 3, SUBLANES, PAIR_W), F32),
            pltpu.VMEM((SUBLANES, LORA_W), F32),
            pltpu.VMEM((tt, LORA_W), F32),
            pltpu.VMEM((ng * pairs, PAIR_W, PAIR_W), F32),
            pair_buf(BF16), pair_buf(BF16), pair_buf(BF16), pair_buf(BF16), pair_buf(BF16),
            pair_buf(F32), pair_buf(F32), pair_buf(F32),
            pltpu.VMEM((pairs, nc, SUBLANES, PAIR_W), F32),
            pltpu.VMEM((pairs, 4, tt, PAIR_W), F32),
        ],
        compiler_params=_cparams(("arbitrary", "arbitrary", "arbitrary")),
        name="rwkv7_mixer",
    )(p3, p3, p3, p3, row(mu), row(mu), row(mu), row(mu),
      row(w0), row(a0), row(k_k), row(k_a), row(r_k), row(ln_g), row(ln_b), w2p, a2p, g2)


def _lru_kernel(x_ref, gt_ref, cw_ref, cb_ref, wr_ref, br_ref, wi_ref, bi_ref, lam_ref, ng_ref,
                y_ref, xtail_ref, h_ref, a_ref, b_ref, *, tt):
    t = pl.program_id(1)
    first = t == 0
    sp = _softplus(-lam_ref[...])
    rowt = lax.broadcasted_iota(jnp.int32, (tt, LRU_BLOCK), 0)
    sub = rowt & (SUBLANES - 1)
    is_start = rowt == jnp.where(first, 0, -1)

    @pl.when(first)
    def _():
        xtail_ref[...] = jnp.zeros_like(xtail_ref)
        h_ref[...] = jnp.zeros_like(h_ref)

    for h in range(LRU_BLOCKS):
        ls = slice(h * LRU_BLOCK, (h + 1) * LRU_BLOCK)
        x = x_ref[0, :, ls]
        prev = xtail_ref[:, ls]
        xc = cb_ref[:, ls] + x * cw_ref[CONV_WIDTH - 1:CONV_WIDTH, ls]
        for j in range(CONV_WIDTH - 1):
            xc = xc + _shift_rows(x, prev, CONV_WIDTH - 1 - j) * cw_ref[j:j + 1, ls]
        xtail_ref[:, ls] = x[tt - SUBLANES:]
        rg = jax.nn.sigmoid(_dot_x3(xc, wr_ref[h]) + br_ref[:, ls])
        ig = jax.nn.sigmoid(_dot_x3(xc, wi_ref[h]) + bi_ref[:, ls])
        log_a = -LRU_C * rg * sp[:, ls]
        a = jnp.exp(log_a)
        th = jnp.tanh(log_a)
        mult = jnp.sqrt(-2.0 * th / (1.0 - th))
        mult = jnp.where(is_start, 1.0, mult)
        bx = mult * ig * xc
        for s in (1, 2, 4):
            keep = sub >= s
            a_s = jnp.where(keep, pltpu.roll(a, s, 0), 1.0)
            b_s = jnp.where(keep, pltpu.roll(bx, s, 0), 0.0)
            bx = a * b_s + bx
            a = a * a_s
        a_ref[:, ls] = a
        b_ref[:, ls] = bx

    def group_step(i, hprev):
        off = pl.multiple_of(i * SUBLANES, SUBLANES)
        hcur = a_ref[pl.ds(off, SUBLANES), :] * hprev + b_ref[pl.ds(off, SUBLANES), :]
        b_ref[pl.ds(off, SUBLANES), :] = hcur
        return jnp.broadcast_to(hcur[SUBLANES - 1:SUBLANES], hcur.shape)

    h_ref[...] = lax.fori_loop(0, tt // SUBLANES, group_step, h_ref[...])

    y = b_ref[...] * jax.nn.gelu(gt_ref[0])
    ms = jnp.mean(y * y, axis=-1, keepdims=True)
    y_ref[0] = (y * lax.rsqrt(ms + NORM_EPS) * ng_ref[...]).astype(y_ref.dtype)


def _lru_mixer(p3, conv_w, conv_b, wr, br, wi, bi, lam, norm_g, tt):
    b, t, _ = p3.shape
    row = lambda a: a.reshape(1, -1)
    full = lambda shape: pl.BlockSpec(shape, lambda bi_, ti: (0,) * len(shape))
    return pl.pallas_call(
        functools.partial(_lru_kernel, tt=tt),
        grid=(b, t // tt),
        in_specs=[
            pl.BlockSpec((1, tt, LRU_W), lambda bi_, ti: (bi_, ti, OFF_LRU_X // LRU_W)),
            pl.BlockSpec((1, tt, LRU_W), lambda bi_, ti: (bi_, ti, OFF_LRU_G // LRU_W)),
            full((CONV_WIDTH, LRU_W)), full((1, LRU_W)),
            full((LRU_BLOCKS, LRU_BLOCK, LRU_BLOCK)), full((1, LRU_W)),
            full((LRU_BLOCKS, LRU_BLOCK, LRU_BLOCK)), full((1, LRU_W)),
            full((1, LRU_W)), full((1, LRU_W)),
        ],
        out_specs=pl.BlockSpec((1, tt, LRU_W), lambda bi_, ti: (bi_, ti, 0)),
        out_shape=jax.ShapeDtypeStruct((b, t, LRU_W), BF16),
        scratch_shapes=[
            pltpu.VMEM((SUBLANES, LRU_W), F32),
            pltpu.VMEM((SUBLANES, LRU_W), F32),
            pltpu.VMEM((tt, LRU_W), F32),
            pltpu.VMEM((tt, LRU_W), F32),
        ],
        compiler_params=_cparams(("arbitrary", "arbitrary")),
        name="rglru_mixer",
    )(p3, p3, conv_w, row(conv_b), wr, row(br), wi, row(bi), row(lam), row(norm_g))


def _outproj_kernel(ya_ref, yb_ref, wa_ref, wb_ref, x_ref, o_ref):
    o_ref[...] = x_ref[...] + _dot(ya_ref[...], wa_ref[...]) + _dot(yb_ref[...], wb_ref[...])


def _outproj(ya, yb, w_bf16, x2d, tm, tn):
    m, d = x2d.shape
    ka = ya.shape[1]
    kb = yb.shape[1]
    return pl.pallas_call(
        _outproj_kernel,
        grid=(m // tm, d // tn),
        in_specs=[
            pl.BlockSpec((tm, ka), lambda i, j: (i, 0)),
            pl.BlockSpec((tm, kb), lambda i, j: (i, 0)),
            pl.BlockSpec((ka, tn), lambda i, j: (0, j)),
            pl.BlockSpec((kb, tn), lambda i, j: (ka // kb, j)),
            pl.BlockSpec((tm, tn), lambda i, j: (i, j)),
        ],
        out_specs=pl.BlockSpec((tm, tn), lambda i, j: (i, j)),
        out_shape=jax.ShapeDtypeStruct((m, d), F32),
        compiler_params=_cparams(("arbitrary", "arbitrary")),
        name="outproj",
    )(ya, yb, w_bf16, w_bf16, x2d)


def _ffn_up_kernel(h_ref, g_ref, wg_ref, wu_ref, o_ref, u_ref):
    @pl.when(pl.program_id(1) == 0)
    def _():
        x = h_ref[...]
        ms = jnp.mean(x * x, axis=-1, keepdims=True)
        u_ref[...] = (x * lax.rsqrt(ms + NORM_EPS) * g_ref[...]).astype(BF16)

    u = u_ref[...]
    gate = _dot(u, wg_ref[...])
    up = _dot(u, wu_ref[...])
    o_ref[...] = (jax.nn.silu(gate) * up).astype(o_ref.dtype)


def _ffn_up(h2d, g, wg_bf16, wu_bf16, tm, tn):
    m, d = h2d.shape
    f = wg_bf16.shape[1]
    return pl.pallas_call(
        _ffn_up_kernel,
        grid=(m // tm, f // tn),
        in_specs=[
            pl.BlockSpec((tm, d), lambda i, j: (i, 0)),
            pl.BlockSpec((1, d), lambda i, j: (0, 0)),
            pl.BlockSpec((d, tn), lambda i, j: (0, j)),
            pl.BlockSpec((d, tn), lambda i, j: (0, j)),
        ],
        out_specs=pl.BlockSpec((tm, tn), lambda i, j: (i, j)),
        out_shape=jax.ShapeDtypeStruct((m, f), BF16),
        scratch_shapes=[pltpu.VMEM((tm, d), BF16)],
        compiler_params=_cparams(("arbitrary", "arbitrary")),
        name="ffn_up",
    )(h2d, g.reshape(1, d), wg_bf16, wu_bf16)


def _ffn_down_kernel(a_ref, w_ref, h_ref, o_ref, acc_ref):
    kk = pl.program_id(2)

    @pl.when(kk == 0)
    def _():
        acc_ref[...] = h_ref[...]

    acc_ref[...] += _dot(a_ref[...], w_ref[...])

    @pl.when(kk == pl.num_programs(2) - 1)
    def _():
        o_ref[...] = acc_ref[...]


def _ffn_down(hid, w_bf16, h2d, tm, tn, tk):
    m, f = hid.shape
    d = w_bf16.shape[1]
    return pl.pallas_call(
        _ffn_down_kernel,
        grid=(m // tm, d // tn, f // tk),
        in_specs=[
            pl.BlockSpec((tm, tk), lambda i, j, kk: (i, kk)),
            pl.BlockSpec((tk, tn), lambda i, j, kk: (kk, j)),
            pl.BlockSpec((tm, tn), lambda i, j, kk: (i, j)),
        ],
        out_specs=pl.BlockSpec((tm, tn), lambda i, j, kk: (i, j)),
        out_shape=jax.ShapeDtypeStruct((m, d), F32),
        scratch_shapes=[pltpu.VMEM((tm, tn), F32)],
        compiler_params=_cparams(("arbitrary", "arbitrary", "arbitrary")),
        name="ffn_down",
    )(hid, w_bf16, h2d)


def _rmsnorm_kernel(x_ref, g_ref, o_ref):
    x = x_ref[...]
    ms = jnp.mean(x * x, axis=-1, keepdims=True)
    o_ref[...] = (x * lax.rsqrt(ms + NORM_EPS) * g_ref[...]).astype(o_ref.dtype)


def _rmsnorm(x2d, g, tm, out_dtype):
    m, d = x2d.shape
    return pl.pallas_call(
        _rmsnorm_kernel,
        grid=(m // tm,),
        in_specs=[pl.BlockSpec((tm, d), lambda i: (i, 0)), pl.BlockSpec((1, d), lambda i: (0, 0))],
        out_specs=pl.BlockSpec((tm, d), lambda i: (i, 0)),
        out_shape=jax.ShapeDtypeStruct((m, d), out_dtype),
        compiler_params=_cparams(("arbitrary",)),
        name="final_rmsnorm",
    )(x2d, g.reshape(1, d))


def _pad_rows(w, rows):
    return jnp.pad(w, ((0, rows - w.shape[0]), (0, 0)))


def _layer(h2d, bsz, seq, norm_mix_g, w_in, mu_shift, rwkv_w0, rwkv_w2, rwkv_a0, rwkv_a2, rwkv_g2,
           rwkv_k_k, rwkv_k_a, rwkv_r_k, rwkv_ln_g, rwkv_ln_b, conv_w, conv_b,
           lru_wr, lru_br, lru_wi, lru_bi, lru_lambda, lru_norm_g, w_out,
           norm_ffn_g, ffn_w_gate, ffn_w_up, ffn_w_down):
    d = h2d.shape[1]
    o1, o2 = 3 * RWKV_W, 3 * RWKV_W + DECAY_LORA
    o3 = o2 + AAA_LORA
    o4 = o3 + GATE_LORA

    def relayout(a, fill):
        padw = [(0, 0)] * (a.ndim - 1) + [(0, LORA_PAD - DECAY_LORA)]
        wd = jnp.pad(a[..., o1:o2], padw, constant_values=fill)
        ad = jnp.pad(a[..., o2:o3], padw, constant_values=fill)
        return jnp.concatenate([a[..., o4:], a[..., :o1], wd, ad, a[..., o3:o4]], axis=-1)

    w_in_p = relayout(w_in, 0.0).astype(BF16)
    mu_full = jnp.concatenate([mu_shift, jnp.zeros((2 * LRU_W,), F32)])
    mu_p = relayout(mu_full, 0.0)
    w2p = _pad_rows(rwkv_w2, LORA_PAD)
    a2p = _pad_rows(rwkv_a2, LORA_PAD)

    p = _norm_matmul(h2d, norm_mix_g, w_in_p, tm=512, tn=512)
    p3 = p.reshape(bsz, seq, P_COLS)
    y_a = _rwkv_mixer(p3, mu_p, rwkv_w0, w2p, rwkv_a0, a2p, rwkv_g2, rwkv_k_k, rwkv_k_a,
                      rwkv_r_k.reshape(-1), rwkv_ln_g, rwkv_ln_b, tt=512, pairs=2)
    y_b = _lru_mixer(p3, conv_w, conv_b, lru_wr, lru_br, lru_wi, lru_bi, lru_lambda, lru_norm_g, tt=256)
    h1 = _outproj(y_a.reshape(-1, RWKV_W), y_b.reshape(-1, LRU_W), w_out.astype(BF16), h2d, tm=512, tn=1024)
    hid = _ffn_up(h1, norm_ffn_g, ffn_w_gate.astype(BF16), ffn_w_up.astype(BF16), tm=512, tn=256)
    return _ffn_down(hid, ffn_w_down.astype(BF16), h1, tm=1024, tn=1024, tk=256)


def kernel(x, norm_mix_g, w_in, mu_shift, rwkv_w0, rwkv_w2, rwkv_a0, rwkv_a2, rwkv_g2, rwkv_k_k, rwkv_k_a, rwkv_r_k, rwkv_ln_g, rwkv_ln_b, conv_w, conv_b, lru_wr, lru_br, lru_wi, lru_bi, lru_lambda, lru_norm_g, w_out, norm_ffn_g, ffn_w_gate, ffn_w_up, ffn_w_down, norm_final_g):
    bsz, seq, d = x.shape
    h = x.astype(F32).reshape(bsz * seq, d)
    depth = w_in.shape[0]
    for l in range(depth):
        h = _layer(h, bsz, seq, norm_mix_g[l], w_in[l], mu_shift[l], rwkv_w0[l], rwkv_w2[l], rwkv_a0[l],
                   rwkv_a2[l], rwkv_g2[l], rwkv_k_k[l], rwkv_k_a[l], rwkv_r_k[l], rwkv_ln_g[l],
                   rwkv_ln_b[l], conv_w[l], conv_b[l], lru_wr[l], lru_br[l], lru_wi[l], lru_bi[l],
                   lru_lambda[l], lru_norm_g[l], w_out[l], norm_ffn_g[l], ffn_w_gate[l], ffn_w_up[l],
                   ffn_w_down[l])
    out = _rmsnorm(h, norm_final_g, tm=256, out_dtype=x.dtype)
    return out.reshape(bsz, seq, d)
```

```python
import functools

import jax
import jax.numpy as jnp
from jax import lax
from jax.experimental import pallas as pl
from jax.experimental.pallas import tpu as pltpu

F32 = jnp.float32
BF16 = jnp.bfloat16

D_MODEL = 4096
RWKV_W = 2048
HEAD = 64
DECAY_LORA = 96
AAA_LORA = 96
GATE_LORA = 256
LRU_W = 2048
LRU_BLOCK = 128
LRU_BLOCKS = LRU_W // LRU_BLOCK
CONV_WIDTH = 4
LRU_C = 8.0
NORM_EPS = 1e-6
GN_EPS = 64e-5

LANES = 128
SUBLANES = 8
LORA_PAD = 128
LORA_W = 2 * LORA_PAD + GATE_LORA
OFF_LRU_X = 0
OFF_LRU_G = LRU_W
OFF_R = 2 * LRU_W
OFF_K = OFF_R + RWKV_W
OFF_V = OFF_K + RWKV_W
OFF_LORA = OFF_V + RWKV_W
P_COLS = OFF_LORA + LORA_W

VMEM_LIMIT = 56 * 1024 * 1024

CHUNK = 64
CHUNK_SHIFT = CHUNK.bit_length() - 1
PAIR_W = 2 * HEAD


def _cparams(sem):
    return pltpu.CompilerParams(dimension_semantics=sem, vmem_limit_bytes=VMEM_LIMIT)


def _dot(a, b):
    return jnp.dot(a, b, preferred_element_type=F32)


def _dot_nt(a, b):
    return lax.dot_general(a, b, (((1,), (1,)), ((), ())), preferred_element_type=F32)


def _split(x):
    hi = x.astype(BF16)
    lo = (x - hi.astype(F32)).astype(BF16)
    return hi, lo


def _dot_hilo(a, b2):
    ah, al = _split(a)
    return _dot(jnp.concatenate([ah, al], axis=1), b2)


def _twice(b):
    return jnp.concatenate([b, b], axis=0)


def _softplus(x):
    return jnp.maximum(x, 0.0) + jnp.log1p(jnp.exp(-jnp.abs(x)))


def _shift_rows(x, prev_tail, s):
    rolled = pltpu.roll(x, s, 0)
    fix = pltpu.roll(prev_tail, s, 0)
    row = lax.broadcasted_iota(jnp.int32, (SUBLANES, x.shape[1]), 0)
    head = jnp.where(row < s, fix, rolled[:SUBLANES])
    return jnp.concatenate([head, rolled[SUBLANES:]], axis=0)


def _norm_matmul_kernel(x_ref, g_ref, w_ref, o_ref, u_ref):
    @pl.when(pl.program_id(1) == 0)
    def _():
        x = x_ref[...]
        ms = jnp.mean(x * x, axis=-1, keepdims=True)
        u_ref[...] = (x * lax.rsqrt(ms + NORM_EPS) * g_ref[...]).astype(BF16)

    o_ref[...] = _dot(u_ref[...], w_ref[...]).astype(o_ref.dtype)


def _norm_matmul(x2d, g, w_bf16, tm, tn):
    m, d = x2d.shape
    n = w_bf16.shape[1]
    return pl.pallas_call(
        _norm_matmul_kernel,
        grid=(m // tm, n // tn),
        in_specs=[
            pl.BlockSpec((tm, d), lambda i, j: (i, 0)),
            pl.BlockSpec((1, d), lambda i, j: (0, 0)),
            pl.BlockSpec((d, tn), lambda i, j: (0, j)),
        ],
        out_specs=pl.BlockSpec((tm, tn), lambda i, j: (i, j)),
        out_shape=jax.ShapeDtypeStruct((m, n), F32),
        scratch_shapes=[pltpu.VMEM((tm, d), BF16)],
        compiler_params=_cparams(("arbitrary", "arbitrary")),
        name="norm_inproj",
    )(x2d, g.reshape(1, d), w_bf16)


def _rwkv_kernel(r_ref, k_ref, v_ref, lo_ref,
                 mur_ref, muk_ref, muv_ref, mulo_ref,
                 w0_ref, a0_ref, kk_ref, ka_ref, rk_ref, lng_ref, lnb_ref,
                 w2_ref, a2_ref, g2_ref,
                 y_ref,
                 tail_ref, lotail_ref, loact_ref, s_ref,
                 qp_ref, mx_ref, sc_ref, y0_ref, g0_ref, el_ref, yraw_ref,
                 *, tt, pairs):
    t = pl.program_id(1)
    g = pl.program_id(2)
    nc = tt // CHUNK
    first = t == 0

    @pl.when(first)
    def _():
        for i in range(pairs * 3):
            tail_ref[g * pairs * 3 + i] = jnp.zeros((SUBLANES, PAIR_W), F32)

    @pl.when(first & (g == 0))
    def _():
        lotail_ref[...] = jnp.zeros_like(lotail_ref)

    @pl.when(g == 0)
    def _():
        lo = lo_ref[0]
        lo_prev = _shift_rows(lo, lotail_ref[...], 1)
        lotail_ref[...] = lo[tt - SUBLANES:]
        xs = lo + (lo_prev - lo) * mulo_ref[...]
        loact_ref[:, 0:LORA_PAD] = jnp.tanh(xs[:, 0:LORA_PAD]).astype(BF16)
        loact_ref[:, LORA_PAD:2 * LORA_PAD] = xs[:, LORA_PAD:2 * LORA_PAD].astype(BF16)
        loact_ref[:, 2 * LORA_PAD:] = jax.nn.sigmoid(xs[:, 2 * LORA_PAD:]).astype(BF16)

    lane = lax.broadcasted_iota(jnp.int32, (1, PAIR_W), 1)
    m0 = lane < HEAD
    row = lax.broadcasted_iota(jnp.int32, (PAIR_W, PAIR_W), 0)
    col = lax.broadcasted_iota(jnp.int32, (PAIR_W, PAIR_W), 1)
    same = (row >> CHUNK_SHIFT) == (col >> CHUNK_SHIFT)
    strict = same & (col < row)
    incl = same & (col <= row)
    eye = (row == col).astype(F32)
    seg_ones2 = _twice(same.astype(BF16))
    tr = lax.broadcasted_iota(jnp.int32, (CHUNK, 2 * CHUNK), 0)
    tc = lax.broadcasted_iota(jnp.int32, (CHUNK, 2 * CHUNK), 1)
    tri2 = ((tc & (CHUNK - 1)) <= tr).astype(BF16)

    def stack(xc):
        return jnp.concatenate([jnp.where(m0, xc, 0.0), jnp.where(m0, 0.0, xc)], axis=0)

    def seg_sum(x):
        return _dot_hilo(x, seg_ones2)

    wdt = loact_ref[:, 0:LORA_PAD]
    adx = loact_ref[:, LORA_PAD:2 * LORA_PAD]
    gds = loact_ref[:, 2 * LORA_PAD:]

    bodies = []
    for p in range(pairs):
        ls = slice(p * PAIR_W, (p + 1) * PAIR_W)
        slot = g * pairs + p

        def shifted(ref, mu_ref, idx):
            x = ref[0, :, ls]
            xp = _shift_rows(x, tail_ref[slot * 3 + idx], 1)
            tail_ref[slot * 3 + idx] = x[tt - SUBLANES:]
            return x + (xp - x) * mu_ref[:, ls]

        r = shifted(r_ref, mur_ref, 0)
        k = shifted(k_ref, muk_ref, 1)
        v = shifted(v_ref, muv_ref, 2)

        wlog = -_softplus(-(w0_ref[:, ls] + _dot(wdt, w2_ref[:, ls]))) - 0.5
        lw = -jnp.exp(wlog)
        asig = jax.nn.sigmoid(a0_ref[:, ls] + _dot(adx, a2_ref[:, ls]))
        gate = _dot(gds, g2_ref[:, ls])

        kk = k * kk_ref[:, ls]
        kk = kk * lax.rsqrt(jnp.maximum(seg_sum(kk * kk), 1e-24))
        k = k * (1.0 + (asig - 1.0) * ka_ref[:, ls])
        a_in = -kk
        b_in = kk * asig

        for c in range(nc):
            cs = slice(c * CHUNK, (c + 1) * CHUNK)
            lwc = lw[cs]
            lwh, lwl = _split(lwc)
            cum = _dot(tri2, jnp.concatenate([lwh, lwl], axis=0))
            cl = cum[CHUNK - 1:CHUNK]
            e_n = jnp.exp(-cum)
            e_l = jnp.exp(cl - cum)
            body = dict(
                at=stack(a_in[cs] * jnp.exp(cum - lwc)).astype(BF16),
                rt=stack(r[cs] * jnp.exp(cum)),
                bk=jnp.concatenate([stack(b_in[cs] * e_n), stack(k[cs] * e_n)], axis=0).astype(BF16),
                bkl=jnp.concatenate([stack(b_in[cs] * e_l), stack(k[cs] * e_l)], axis=0).astype(BF16),
                vs=stack(v[cs]),
            )
            el_ref[p, c] = jnp.broadcast_to(jnp.exp(cl), (SUBLANES, PAIR_W))
            bodies.append((p, c, body))

        yraw_ref[p, 1] = r * k * rk_ref[:, ls]
        yraw_ref[p, 2] = v
        yraw_ref[p, 3] = gate

    strict2 = jnp.concatenate([strict, strict], axis=1)
    incl2 = jnp.concatenate([incl, incl], axis=1)
    for _, _, bd in bodies:
        sa = jnp.where(strict2, _dot_nt(bd["at"], bd["bk"]), 0.0)
        sr = jnp.where(incl2, _dot_nt(bd["rt"].astype(BF16), bd["bk"]), 0.0)
        bd["n"] = sa[:, :PAIR_W]
        bd["aak"] = sa[:, PAIR_W:].astype(BF16)
        bd["arb"] = sr[:, :PAIR_W].astype(BF16)
        bd["ark"] = sr[:, PAIR_W:].astype(BF16)
        bd["vsb"] = bd["vs"].astype(BF16)
    for _, _, bd in bodies:
        bd["nb"] = bd["n"].astype(BF16)
        bd["t"] = eye + bd["n"]
    for _, _, bd in bodies:
        bd["nb_next"] = _dot(bd["nb"], bd["nb"]).astype(BF16)
    for _ in range(CHUNK_SHIFT - 2):
        for _, _, bd in bodies:
            nb = bd["nb_next"]
            res = _dot(nb, jnp.concatenate([nb, bd["t"].astype(BF16)], axis=1))
            bd["nb_next"] = res[:, :PAIR_W].astype(BF16)
            bd["t"] = bd["t"] + res[:, PAIR_W:]
    for _, _, bd in bodies:
        bd["t"] = bd["t"] + _dot(bd["nb_next"], bd["t"].astype(BF16))
        bd["x0"] = _dot(bd["aak"], bd["vsb"])
    for _, _, bd in bodies:
        bd["wu"] = _dot(bd["t"].astype(BF16), jnp.concatenate([bd["at"], bd["x0"].astype(BF16)], axis=1))
    for p, c, bd in bodies:
        wub = bd["wu"].astype(BF16)
        qy = _dot(bd["arb"], wub)
        qp_ref[p, c] = (bd["rt"] + qy[:, :PAIR_W]).astype(BF16)
        y0_ref[p, c] = qy[:, PAIR_W:] + _dot(bd["ark"], bd["vsb"])
    for p, c, bd in bodies:
        wut = bd["wu"].T
        mx_ref[p, c] = _dot(wut[:PAIR_W].astype(BF16), bd["bkl"][:PAIR_W]).astype(BF16)
        uv_t = jnp.concatenate([wut[PAIR_W:], bd["vs"].T], axis=1).astype(BF16)
        g0_ref[p, c] = _dot(uv_t, bd["bkl"])

    @pl.when(first)
    def _():
        for p in range(pairs):
            s_ref[g * pairs + p] = jnp.zeros((PAIR_W, PAIR_W), F32)

    def chunk_step(c, carry):
        for p in range(pairs):
            slot = g * pairs + p
            s = s_ref[slot]
            sb = s.astype(BF16)
            sc_ref[p, c] = sb
            s_ref[slot] = s * el_ref[p, c][0:1] + _dot(sb, mx_ref[p, c]) + g0_ref[p, c]
        return carry

    lax.fori_loop(0, nc, chunk_step, 0)

    for p in range(pairs):
        for c in range(nc):
            y2 = _dot_nt(qp_ref[p, c], sc_ref[p, c]) + y0_ref[p, c]
            yraw_ref[p, 0, c * CHUNK:(c + 1) * CHUNK, :] = y2[:CHUNK] + y2[CHUNK:]

    for p in range(pairs):
        ls = slice(p * PAIR_W, (p + 1) * PAIR_W)
        y = yraw_ref[p, 0]
        mean = seg_sum(y) * (1.0 / HEAD)
        yc = y - mean
        var = seg_sum(yc * yc) * (1.0 / HEAD)
        yn = yc * lax.rsqrt(var + GN_EPS) * lng_ref[:, ls] + lnb_ref[:, ls]
        bonus = seg_sum(yraw_ref[p, 1])
        out = (yn + bonus * yraw_ref[p, 2]) * yraw_ref[p, 3]
        y_ref[0, :, ls] = out.astype(y_ref.dtype)


def _rwkv_mixer(p3, mu, w0, w2p, a0, a2p, g2, k_k, k_a, r_k, ln_g, ln_b, tt, pairs):
    b, t, _ = p3.shape
    gw = pairs * PAIR_W
    ng = RWKV_W // gw
    nc = tt // CHUNK
    row = lambda a: a.reshape(1, -1)
    col_spec = lambda off: pl.BlockSpec((1, gw), lambda bi, ti, gi, off=off: (0, off // gw + gi))
    act_spec = lambda off: pl.BlockSpec((1, tt, gw), lambda bi, ti, gi, off=off: (bi, ti, off // gw + gi))
    kern = functools.partial(_rwkv_kernel, tt=tt, pairs=pairs)
    pair_buf = lambda dt: pltpu.VMEM((pairs, nc, PAIR_W, PAIR_W), dt)
    return pl.pallas_call(
        kern,
        grid=(b, t // tt, ng),
        in_specs=[
            act_spec(OFF_R), act_spec(OFF_K), act_spec(OFF_V),
            pl.BlockSpec((1, tt, LORA_W), lambda bi, ti, gi: (bi, ti, OFF_LORA // LORA_W)),
            col_spec(OFF_R), col_spec(OFF_K), col_spec(OFF_V),
            pl.BlockSpec((1, LORA_W), lambda bi, ti, gi: (0, OFF_LORA // LORA_W)),
            col_spec(0), col_spec(0), col_spec(0), col_spec(0), col_spec(0), col_spec(0), col_spec(0),
            pl.BlockSpec((LORA_PAD, gw), lambda bi, ti, gi: (0, gi)),
            pl.BlockSpec((LORA_PAD, gw), lambda bi, ti, gi: (0, gi)),
            pl.BlockSpec((GATE_LORA, gw), lambda bi, ti, gi: (0, gi)),
        ],
        out_specs=pl.BlockSpec((1, tt, gw), lambda bi, ti, gi: (bi, ti, gi)),
        out_shape=jax.ShapeDtypeStruct((b, t, RWKV_W), BF16),
        scratch_shapes=[
            pltpu.VMEM((ng * pairs * 3, SUBLANES, PAIR_W), F32),
            pltpu.VMEM((SUBLANES, LORA_W), F32),
            pltpu.VMEM((tt, LORA_W), BF16),
            pltpu.VMEM((ng * pairs, PAIR_W, PAIR_W), F32),
            pair_buf(BF16), pair_buf(BF16), pair_buf(BF16),
            pair_buf(F32), pair_buf(F32),
            pltpu.VMEM((pairs, nc, SUBLANES, PAIR_W), F32),
            pltpu.VMEM((pairs, 4, tt, PAIR_W), F32),
        ],
        compiler_params=_cparams(("arbitrary", "arbitrary", "arbitrary")),
        name="rwkv7_mixer",
    )(p3, p3, p3, p3, row(mu), row(mu), row(mu), row(mu),
      row(w0), row(a0), row(k_k), row(k_a), row(r_k), row(ln_g), row(ln_b),
      w2p.astype(BF16), a2p.astype(BF16), g2.astype(BF16))


def _lru_kernel(x_ref, gt_ref, cw_ref, cb_ref, wr_ref, br_ref, wi_ref, bi_ref, lam_ref, ng_ref,
                y_ref, xtail_ref, h_ref, a_ref, b_ref, *, tt):
    t = pl.program_id(1)
    first = t == 0
    sp = _softplus(-lam_ref[...])
    rowt = lax.broadcasted_iota(jnp.int32, (tt, LRU_BLOCK), 0)
    sub = rowt & (SUBLANES - 1)
    is_start = rowt == jnp.where(first, 0, -1)

    @pl.when(first)
    def _():
        xtail_ref[...] = jnp.zeros_like(xtail_ref)
        h_ref[...] = jnp.zeros_like(h_ref)

    for h in range(LRU_BLOCKS):
        ls = slice(h * LRU_BLOCK, (h + 1) * LRU_BLOCK)
        x = x_ref[0, :, ls]
        prev = xtail_ref[:, ls]
        xc = cb_ref[:, ls] + x * cw_ref[CONV_WIDTH - 1:CONV_WIDTH, ls]
        for j in range(CONV_WIDTH - 1):
            xc = xc + _shift_rows(x, prev, CONV_WIDTH - 1 - j) * cw_ref[j:j + 1, ls]
        xtail_ref[:, ls] = x[tt - SUBLANES:]
        xh, xl = _split(xc)
        x2 = jnp.concatenate([xh, xl], axis=1)
        rg = jax.nn.sigmoid(_dot(x2, wr_ref[h]) + br_ref[:, ls])
        ig = jax.nn.sigmoid(_dot(x2, wi_ref[h]) + bi_ref[:, ls])
        log_a = -LRU_C * rg * sp[:, ls]
        a = jnp.exp(log_a)
        th = jnp.tanh(log_a)
        mult = jnp.sqrt(-2.0 * th / (1.0 - th))
        mult = jnp.where(is_start, 1.0, mult)
        bx = mult * ig * xc
        for s in (1, 2, 4):
            keep = sub >= s
            a_s = jnp.where(keep, pltpu.roll(a, s, 0), 1.0)
            b_s = jnp.where(keep, pltpu.roll(bx, s, 0), 0.0)
            bx = a * b_s + bx
            a = a * a_s
        a_ref[:, ls] = a
        b_ref[:, ls] = bx

    def group_step(i, hprev):
        off = pl.multiple_of(i * SUBLANES, SUBLANES)
        hcur = a_ref[pl.ds(off, SUBLANES), :] * hprev + b_ref[pl.ds(off, SUBLANES), :]
        b_ref[pl.ds(off, SUBLANES), :] = hcur
        return jnp.broadcast_to(hcur[SUBLANES - 1:SUBLANES], hcur.shape)

    h_ref[...] = lax.fori_loop(0, tt // SUBLANES, group_step, h_ref[...])

    y = b_ref[...] * jax.nn.gelu(gt_ref[0])
    ms = jnp.mean(y * y, axis=-1, keepdims=True)
    y_ref[0] = (y * lax.rsqrt(ms + NORM_EPS) * ng_ref[...]).astype(y_ref.dtype)


def _lru_mixer(p3, conv_w, conv_b, wr, br, wi, bi, lam, norm_g, tt):
    b, t, _ = p3.shape
    row = lambda a: a.reshape(1, -1)
    full = lambda shape: pl.BlockSpec(shape, lambda bi_, ti: (0,) * len(shape))
    twice = lambda w: jnp.concatenate([w, w], axis=1).astype(BF16)
    return pl.pallas_call(
        functools.partial(_lru_kernel, tt=tt),
        grid=(b, t // tt),
        in_specs=[
            pl.BlockSpec((1, tt, LRU_W), lambda bi_, ti: (bi_, ti, OFF_LRU_X // LRU_W)),
            pl.BlockSpec((1, tt, LRU_W), lambda bi_, ti: (bi_, ti, OFF_LRU_G // LRU_W)),
            full((CONV_WIDTH, LRU_W)), full((1, LRU_W)),
            full((LRU_BLOCKS, 2 * LRU_BLOCK, LRU_BLOCK)), full((1, LRU_W)),
            full((LRU_BLOCKS, 2 * LRU_BLOCK, LRU_BLOCK)), full((1, LRU_W)),
            full((1, LRU_W)), full((1, LRU_W)),
        ],
        out_specs=pl.BlockSpec((1, tt, LRU_W), lambda bi_, ti: (bi_, ti, 0)),
        out_shape=jax.ShapeDtypeStruct((b, t, LRU_W), BF16),
        scratch_shapes=[
            pltpu.VMEM((SUBLANES, LRU_W), F32),
            pltpu.VMEM((SUBLANES, LRU_W), F32),
            pltpu.VMEM((tt, LRU_W), F32),
            pltpu.VMEM((tt, LRU_W), F32),
        ],
        compiler_params=_cparams(("arbitrary", "arbitrary")),
        name="rglru_mixer",
    )(p3, p3, conv_w, row(conv_b), twice(wr), row(br), twice(wi), row(bi), row(lam), row(norm_g))


def _outproj_kernel(ya_ref, yb_ref, wa_ref, wb_ref, x_ref, o_ref):
    o_ref[...] = x_ref[...] + _dot(ya_ref[...], wa_ref[...]) + _dot(yb_ref[...], wb_ref[...])


def _outproj(ya, yb, w_bf16, x2d, tm, tn):
    m, d = x2d.shape
    ka = ya.shape[1]
    kb = yb.shape[1]
    return pl.pallas_call(
        _outproj_kernel,
        grid=(m // tm, d // tn),
        in_specs=[
            pl.BlockSpec((tm, ka), lambda i, j: (i, 0)),
            pl.BlockSpec((tm, kb), lambda i, j: (i, 0)),
            pl.BlockSpec((ka, tn), lambda i, j: (0, j)),
            pl.BlockSpec((kb, tn), lambda i, j: (ka // kb, j)),
            pl.BlockSpec((tm, tn), lambda i, j: (i, j)),
        ],
        out_specs=pl.BlockSpec((tm, tn), lambda i, j: (i, j)),
        out_shape=jax.ShapeDtypeStruct((m, d), F32),
        compiler_params=_cparams(("arbitrary", "arbitrary")),
        name="outproj",
    )(ya, yb, w_bf16, w_bf16, x2d)


def _ffn_up_kernel(h_ref, g_ref, wg_ref, wu_ref, o_ref, u_ref):
    @pl.when(pl.program_id(1) == 0)
    def _():
        x = h_ref[...]
        ms = jnp.mean(x * x, axis=-1, keepdims=True)
        u_ref[...] = (x * lax.rsqrt(ms + NORM_EPS) * g_ref[...]).astype(BF16)

    u = u_ref[...]
    gate = _dot(u, wg_ref[...])
    up = _dot(u, wu_ref[...])
    o_ref[...] = (jax.nn.silu(gate) * up).astype(o_ref.dtype)


def _ffn_up(h2d, g, wg_bf16, wu_bf16, tm, tn):
    m, d = h2d.shape
    f = wg_bf16.shape[1]
    return pl.pallas_call(
        _ffn_up_kernel,
        grid=(m // tm, f // tn),
        in_specs=[
            pl.BlockSpec((tm, d), lambda i, j: (i, 0)),
            pl.BlockSpec((1, d), lambda i, j: (0, 0)),
            pl.BlockSpec((d, tn), lambda i, j: (0, j)),
            pl.BlockSpec((d, tn), lambda i, j: (0, j)),
        ],
        out_specs=pl.BlockSpec((tm, tn), lambda i, j: (i, j)),
        out_shape=jax.ShapeDtypeStruct((m, f), BF16),
        scratch_shapes=[pltpu.VMEM((tm, d), BF16)],
        compiler_params=_cparams(("arbitrary", "arbitrary")),
        name="ffn_up",
    )(h2d, g.reshape(1, d), wg_bf16, wu_bf16)


def _ffn_down_kernel(a_ref, w_ref, h_ref, g_ref, o_ref, *, final_norm):
    kk = pl.program_id(1)

    @pl.when(kk == 0)
    def _():
        o_ref[...] = h_ref[...]

    o_ref[...] += _dot(a_ref[...], w_ref[...])

    if final_norm:
        @pl.when(kk == pl.num_programs(1) - 1)
        def _():
            x = o_ref[...]
            ms = jnp.mean(x * x, axis=-1, keepdims=True)
            o_ref[...] = x * lax.rsqrt(ms + NORM_EPS) * g_ref[...]


def _ffn_down(hid, w_bf16, h2d, final_g, tm, tk):
    m, f = hid.shape
    d = w_bf16.shape[1]
    final_norm = final_g is not None
    g = final_g if final_norm else jnp.ones((d,), F32)
    return pl.pallas_call(
        functools.partial(_ffn_down_kernel, final_norm=final_norm),
        grid=(m // tm, f // tk),
        in_specs=[
            pl.BlockSpec((tm, tk), lambda i, kk: (i, kk)),
            pl.BlockSpec((tk, d), lambda i, kk: (kk, 0)),
            pl.BlockSpec((tm, d), lambda i, kk: (i, 0)),
            pl.BlockSpec((1, d), lambda i, kk: (0, 0)),
        ],
        out_specs=pl.BlockSpec((tm, d), lambda i, kk: (i, 0)),
        out_shape=jax.ShapeDtypeStruct((m, d), F32),
        compiler_params=_cparams(("arbitrary", "arbitrary")),
        name="ffn_down",
    )(hid, w_bf16, h2d, g.reshape(1, d))


FF_TILE = 512


def _pad_rows(w, rows):
    return jnp.pad(w, ((0, rows - w.shape[0]), (0, 0)))


def _layer(h2d, bsz, seq, final_g, norm_mix_g, w_in, mu_shift, rwkv_w0, rwkv_w2, rwkv_a0, rwkv_a2,
           rwkv_g2, rwkv_k_k, rwkv_k_a, rwkv_r_k, rwkv_ln_g, rwkv_ln_b, conv_w, conv_b,
           lru_wr, lru_br, lru_wi, lru_bi, lru_lambda, lru_norm_g, w_out,
           norm_ffn_g, ffn_w_gate, ffn_w_up, ffn_w_down):
    d = h2d.shape[1]
    o1, o2 = 3 * RWKV_W, 3 * RWKV_W + DECAY_LORA
    o3 = o2 + AAA_LORA
    o4 = o3 + GATE_LORA

    def relayout(a, fill):
        padw = [(0, 0)] * (a.ndim - 1) + [(0, LORA_PAD - DECAY_LORA)]
        wd = jnp.pad(a[..., o1:o2], padw, constant_values=fill)
        ad = jnp.pad(a[..., o2:o3], padw, constant_values=fill)
        return jnp.concatenate([a[..., o4:], a[..., :o1], wd, ad, a[..., o3:o4]], axis=-1)

    w_in_p = relayout(w_in, 0.0).astype(BF16)
    mu_full = jnp.concatenate([mu_shift, jnp.zeros((2 * LRU_W,), F32)])
    mu_p = relayout(mu_full, 0.0)
    w2p = _pad_rows(rwkv_w2, LORA_PAD)
    a2p = _pad_rows(rwkv_a2, LORA_PAD)

    p = _norm_matmul(h2d, norm_mix_g, w_in_p, tm=512, tn=512)
    p3 = p.reshape(bsz, seq, P_COLS)
    y_a = _rwkv_mixer(p3, mu_p, rwkv_w0, w2p, rwkv_a0, a2p, rwkv_g2, rwkv_k_k, rwkv_k_a,
                      rwkv_r_k.reshape(-1), rwkv_ln_g, rwkv_ln_b, tt=512, pairs=2)
    y_b = _lru_mixer(p3, conv_w, conv_b, lru_wr, lru_br, lru_wi, lru_bi, lru_lambda, lru_norm_g, tt=256)
    h1 = _outproj(y_a.reshape(-1, RWKV_W), y_b.reshape(-1, LRU_W), w_out.astype(BF16), h2d, tm=512, tn=1024)
    f = ffn_w_gate.shape[1]
    fpad = -f % FF_TILE
    wg = jnp.pad(ffn_w_gate, ((0, 0), (0, fpad))).astype(BF16)
    wu = jnp.pad(ffn_w_up, ((0, 0), (0, fpad))).astype(BF16)
    wd = jnp.pad(ffn_w_down, ((0, fpad), (0, 0))).astype(BF16)
    hid = _ffn_up(h1, norm_ffn_g, wg, wu, tm=512, tn=FF_TILE)
    return _ffn_down(hid, wd, h1, final_g, tm=512, tk=FF_TILE)


def kernel(x, norm_mix_g, w_in, mu_shift, rwkv_w0, rwkv_w2, rwkv_a0, rwkv_a2, rwkv_g2, rwkv_k_k, rwkv_k_a, rwkv_r_k, rwkv_ln_g, rwkv_ln_b, conv_w, conv_b, lru_wr, lru_br, lru_wi, lru_bi, lru_lambda, lru_norm_g, w_out, norm_ffn_g, ffn_w_gate, ffn_w_up, ffn_w_down, norm_final_g):
    bsz, seq, d = x.shape
    h = x.astype(F32).reshape(bsz * seq, d)
    depth = w_in.shape[0]
    for l in range(depth):
        last = l == depth - 1
        h = _layer(h, bsz, seq, norm_final_g if last else None, norm_mix_g[l], w_in[l], mu_shift[l], rwkv_w0[l], rwkv_w2[l], rwkv_a0[l],
                   rwkv_a2[l], rwkv_g2[l], rwkv_k_k[l], rwkv_k_a[l], rwkv_r_k[l], rwkv_ln_g[l],
                   rwkv_ln_b[l], conv_w[l], conv_b[l], lru_wr[l], lru_br[l], lru_wi[l], lru_bi[l],
                   lru_lambda[l], lru_norm_g[l], w_out[l], norm_ffn_g[l], ffn_w_gate[l], ffn_w_up[l],
                   ffn_w_down[l])
    return h.reshape(bsz, seq, d).astype(x.dtype)
```

```python
import functools

import jax
import jax.numpy as jnp
import numpy as np
from jax import lax
from jax.experimental import pallas as pl
from jax.experimental.pallas import tpu as pltpu

F32 = jnp.float32
BF16 = jnp.bfloat16

D_MODEL = 4096
RWKV_W = 2048
HEAD = 64
DECAY_LORA = 96
AAA_LORA = 96
GATE_LORA = 256
LRU_W = 2048
LRU_BLOCK = 128
LRU_BLOCKS = LRU_W // LRU_BLOCK
CONV_WIDTH = 4
LRU_C = 8.0
NORM_EPS = 1e-6
GN_EPS = 64e-5

LANES = 128
SUBLANES = 8
LORA_PAD = 128
LORA_W = 2 * LORA_PAD + GATE_LORA
OFF_LRU_X = 0
OFF_LRU_G = LRU_W
OFF_R = 2 * LRU_W
OFF_K = OFF_R + RWKV_W
OFF_V = OFF_K + RWKV_W
OFF_LORA = OFF_V + RWKV_W
P_COLS = OFF_LORA + LORA_W

VMEM_LIMIT = 56 * 1024 * 1024

CHUNK = 64
CHUNK_SHIFT = CHUNK.bit_length() - 1
PAIR_W = 2 * HEAD
STREAM_SKEW = 9


def _cparams(sem):
    return pltpu.CompilerParams(dimension_semantics=sem, vmem_limit_bytes=VMEM_LIMIT)


def _dot(a, b):
    return jnp.dot(a, b, preferred_element_type=F32)


def _dot_nt(a, b):
    return lax.dot_general(a, b, (((1,), (1,)), ((), ())), preferred_element_type=F32)


def _split(x):
    hi = x.astype(BF16)
    lo = (x - hi.astype(F32)).astype(BF16)
    return hi, lo


def _dot_hilo(a, b2):
    ah, al = _split(a)
    return _dot(jnp.concatenate([ah, al], axis=1), b2)


def _softplus(x):
    return jnp.maximum(x, 0.0) + jnp.log1p(jnp.exp(-jnp.abs(x)))


def _shift_rows(x, prev_tail, s):
    rolled = pltpu.roll(x, s, 0)
    fix = pltpu.roll(prev_tail, s, 0)
    row = lax.broadcasted_iota(jnp.int32, (SUBLANES, x.shape[1]), 0)
    head = jnp.where(row < s, fix, rolled[:SUBLANES])
    return jnp.concatenate([head, rolled[SUBLANES:]], axis=0)


def _norm_matmul_kernel(x_ref, g_ref, w_ref, o_ref, u_ref):
    @pl.when(pl.program_id(1) == 0)
    def _():
        x = x_ref[...]
        ms = jnp.mean(x * x, axis=-1, keepdims=True)
        u_ref[...] = (x * lax.rsqrt(ms + NORM_EPS) * g_ref[...]).astype(BF16)

    o_ref[...] = _dot(u_ref[...], w_ref[...]).astype(o_ref.dtype)


def _norm_matmul(x2d, g, w_bf16, tm, tn):
    m, d = x2d.shape
    n = w_bf16.shape[1]
    return pl.pallas_call(
        _norm_matmul_kernel,
        grid=(m // tm, n // tn),
        in_specs=[
            pl.BlockSpec((tm, d), lambda i, j: (i, 0), pipeline_mode=pl.Buffered(1)),
            pl.BlockSpec((1, d), lambda i, j: (0, 0)),
            pl.BlockSpec((d, tn), lambda i, j: (0, j)),
        ],
        out_specs=pl.BlockSpec((tm, tn), lambda i, j: (i, j)),
        out_shape=jax.ShapeDtypeStruct((m, n), F32),
        scratch_shapes=[pltpu.VMEM((tm, d), BF16)],
        compiler_params=_cparams(("arbitrary", "arbitrary")),
        name="norm_inproj",
    )(x2d, g.reshape(1, d), w_bf16)


def _rwkv_kernel(r_ref, k_ref, v_ref, lo_ref,
                 mur_ref, muk_ref, muv_ref, mulo_ref,
                 w0_ref, a0_ref, kk_ref, ka_ref, rk_ref, lng_ref, lnb_ref,
                 w2_ref, a2_ref, g2_ref, mask_ref, seg_ref, tri_ref,
                 y_ref,
                 tail_ref, lotail_ref, loact_ref, s_ref, yraw_ref,
                 *, tt, pairs):
    t = pl.program_id(1)
    g = pl.program_id(2)
    nc = tt // CHUNK
    first = t == 0

    @pl.when(first)
    def _():
        for i in range(pairs * 3):
            tail_ref[g * pairs * 3 + i] = jnp.zeros((SUBLANES, PAIR_W), F32)

    @pl.when(first & (g == 0))
    def _():
        lotail_ref[...] = jnp.zeros_like(lotail_ref)

    @pl.when(g == 0)
    def _():
        lo = lo_ref[0]
        lo_prev = _shift_rows(lo, lotail_ref[...], 1)
        lotail_ref[...] = lo[tt - SUBLANES:]
        xs = lo + (lo_prev - lo) * mulo_ref[...]
        loact_ref[:, 0:LORA_PAD] = jnp.tanh(xs[:, 0:LORA_PAD]).astype(BF16)
        loact_ref[:, LORA_PAD:2 * LORA_PAD] = xs[:, LORA_PAD:2 * LORA_PAD].astype(BF16)
        loact_ref[:, 2 * LORA_PAD:] = jax.nn.sigmoid(xs[:, 2 * LORA_PAD:]).astype(BF16)

    lane = lax.broadcasted_iota(jnp.int32, (1, PAIR_W), 1)
    m0 = lane < HEAD
    strict2 = mask_ref[:, 0:2 * PAIR_W] > 0.5
    incl2 = mask_ref[:, 2 * PAIR_W:4 * PAIR_W] > 0.5
    eye = mask_ref[:, 4 * PAIR_W:]
    tri2 = tri_ref[...]

    def stack(xc):
        return jnp.concatenate([jnp.where(m0, xc, 0.0), jnp.where(m0, 0.0, xc)], axis=0)

    def seg_sum(x):
        return _dot_hilo(x, seg_ref[...])

    def seg_sum_1pass(x):
        return _dot(x.astype(BF16), seg_ref[0:PAIR_W])

    wdt = loact_ref[:, 0:LORA_PAD]
    adx = loact_ref[:, LORA_PAD:2 * LORA_PAD]
    gds = loact_ref[:, 2 * LORA_PAD:]

    def pair_stream(p):
        bodies = []
        ls = slice(p * PAIR_W, (p + 1) * PAIR_W)
        slot = g * pairs + p

        def shifted(ref, mu_ref, idx):
            x = ref[0, :, ls]
            xp = _shift_rows(x, tail_ref[slot * 3 + idx], 1)
            tail_ref[slot * 3 + idx] = x[tt - SUBLANES:]
            return x + (xp - x) * mu_ref[:, ls]

        r = shifted(r_ref, mur_ref, 0)
        k = shifted(k_ref, muk_ref, 1)
        v = shifted(v_ref, muv_ref, 2)

        wlog = -_softplus(-(w0_ref[:, ls] + _dot(wdt, w2_ref[:, ls]))) - 0.5
        lw = -jnp.exp(wlog)
        asig = jax.nn.sigmoid(a0_ref[:, ls] + _dot(adx, a2_ref[:, ls]))
        gate = _dot(gds, g2_ref[:, ls])

        kk = k * kk_ref[:, ls]
        kk = kk * lax.rsqrt(jnp.maximum(seg_sum(kk * kk), 1e-24))
        k = k * (1.0 + (asig - 1.0) * ka_ref[:, ls])
        a_in = -kk
        b_in = kk * asig
        yraw_ref[p, 1] = r * k * rk_ref[:, ls]
        yraw_ref[p, 2] = v
        yraw_ref[p, 3] = gate
        yield

        for c in range(nc):
            cs = slice(c * CHUNK, (c + 1) * CHUNK)
            lwc = lw[cs]
            lwh, lwl = _split(lwc)
            cum = _dot(tri2, jnp.concatenate([lwh, lwl], axis=0))
            cl = cum[CHUNK - 1:CHUNK]
            e_n = jnp.exp(-cum)
            e_l = jnp.exp(cl - cum)
            body = dict(
                at=stack(a_in[cs] * jnp.exp(cum - lwc)).astype(BF16),
                rt=stack(r[cs] * jnp.exp(cum)),
                bk=jnp.concatenate([stack(b_in[cs] * e_n), stack(k[cs] * e_n)], axis=0).astype(BF16),
                bkl=jnp.concatenate([stack(b_in[cs] * e_l), stack(k[cs] * e_l)], axis=0).astype(BF16),
                vs=stack(v[cs]),
                el=jnp.exp(cl),
            )
            bodies.append(body)
            yield

        for bd in bodies:
            sa = jnp.where(strict2, _dot_nt(bd["at"], bd["bk"]), 0.0)
            sr = jnp.where(incl2, _dot_nt(bd["rt"].astype(BF16), bd["bk"]), 0.0)
            bd["n"] = sa[:, :PAIR_W]
            bd["aak"] = sa[:, PAIR_W:].astype(BF16)
            bd["arb"] = sr[:, :PAIR_W].astype(BF16)
            bd["ark"] = sr[:, PAIR_W:].astype(BF16)
            bd["vsb"] = bd["vs"].astype(BF16)
        yield
        for bd in bodies:
            nb = bd["n"].astype(BF16)
            bd["t"] = eye + bd["n"]
            bd["nb"] = _dot(nb, nb).astype(BF16)
        yield
        for _ in range(CHUNK_SHIFT - 2):
            for bd in bodies:
                nb = bd["nb"]
                res = _dot(nb, jnp.concatenate([nb, bd["t"].astype(BF16)], axis=1))
                bd["nb"] = res[:, :PAIR_W].astype(BF16)
                bd["t"] = bd["t"] + res[:, PAIR_W:]
            yield
        for bd in bodies:
            bd["t"] = bd["t"] + _dot(bd["nb"], bd["t"].astype(BF16))
            bd["x0"] = _dot(bd["aak"], bd["vsb"])
        yield
        for bd in bodies:
            bd["wu"] = _dot(bd["t"].astype(BF16), jnp.concatenate([bd["at"], bd["x0"].astype(BF16)], axis=1))
        yield
        for bd in bodies:
            qy = _dot(bd["arb"], bd["wu"].astype(BF16))
            bd["qp"] = (bd["rt"] + qy[:, :PAIR_W]).astype(BF16)
            bd["y0"] = qy[:, PAIR_W:] + _dot(bd["ark"], bd["vsb"])
        yield
        for bd in bodies:
            wut = bd["wu"].T
            bd["mx"] = _dot(wut[:PAIR_W].astype(BF16), bd["bkl"][:PAIR_W]).astype(BF16)
            uv_t = jnp.concatenate([wut[PAIR_W:], bd["vs"].T], axis=1).astype(BF16)
            bd["g0"] = _dot(uv_t, bd["bkl"])
        yield

        s = s_ref[slot]
        for bd in bodies:
            bd["s_in"] = s.astype(BF16)
            s = s * bd["el"] + _dot(bd["s_in"], bd["mx"]) + bd["g0"]
            yield
        s_ref[slot] = s
        for c, bd in enumerate(bodies):
            y2 = _dot_nt(bd["qp"], bd["s_in"]) + bd["y0"]
            yraw_ref[p, 0, c * CHUNK:(c + 1) * CHUNK, :] = y2[:CHUNK] + y2[CHUNK:]
        yield

        y = yraw_ref[p, 0]
        mean = seg_sum_1pass(y) * (1.0 / HEAD)
        yc = y - mean
        var = seg_sum(yc * yc) * (1.0 / HEAD)
        yn = yc * lax.rsqrt(var + GN_EPS) * lng_ref[:, ls] + lnb_ref[:, ls]
        bonus = seg_sum_1pass(yraw_ref[p, 1])
        out = (yn + bonus * yraw_ref[p, 2]) * yraw_ref[p, 3]
        y_ref[0, :, ls] = out.astype(y_ref.dtype)
        yield

    @pl.when(first)
    def _():
        for p in range(pairs):
            s_ref[g * pairs + p] = jnp.zeros((PAIR_W, PAIR_W), F32)

    pending = [pair_stream(p) for p in range(pairs)]
    live = []
    tick = 0
    while pending or live:
        if pending and tick % STREAM_SKEW == 0:
            live.append(pending.pop(0))
        for stream in list(live):
            if next(stream, "done") == "done":
                live.remove(stream)
        tick += 1


def _rwkv_constants():
    idx = np.arange(PAIR_W)
    same = (idx[:, None] // CHUNK) == (idx[None, :] // CHUNK)
    strict = same & (idx[None, :] < idx[:, None])
    incl = same & (idx[None, :] <= idx[:, None])
    masks = np.concatenate([strict, strict, incl, incl, np.eye(PAIR_W, dtype=bool)], axis=1)
    seg2 = np.concatenate([same, same], axis=0)
    tr = np.arange(CHUNK)
    tri = tr[None, :] <= tr[:, None]
    tri2 = np.concatenate([tri, tri], axis=1)
    return (jnp.asarray(masks, F32), jnp.asarray(seg2, BF16), jnp.asarray(tri2, BF16))


def _rwkv_mixer(p3, mu, w0, w2p, a0, a2p, g2, k_k, k_a, r_k, ln_g, ln_b, tt, pairs):
    b, t, _ = p3.shape
    gw = pairs * PAIR_W
    ng = RWKV_W // gw
    row = lambda a: a.reshape(1, -1)
    col_spec = lambda off: pl.BlockSpec((1, gw), lambda bi, ti, gi, off=off: (0, off // gw + gi))
    act_spec = lambda off: pl.BlockSpec((1, tt, gw), lambda bi, ti, gi, off=off: (bi, ti, off // gw + gi))
    const_spec = lambda a: pl.BlockSpec(a.shape, lambda bi, ti, gi: (0, 0))
    kern = functools.partial(_rwkv_kernel, tt=tt, pairs=pairs)
    masks, seg2, tri2 = _rwkv_constants()
    return pl.pallas_call(
        kern,
        grid=(b, t // tt, ng),
        in_specs=[
            act_spec(OFF_R), act_spec(OFF_K), act_spec(OFF_V),
            pl.BlockSpec((1, tt, LORA_W), lambda bi, ti, gi: (bi, ti, OFF_LORA // LORA_W)),
            col_spec(OFF_R), col_spec(OFF_K), col_spec(OFF_V),
            pl.BlockSpec((1, LORA_W), lambda bi, ti, gi: (0, OFF_LORA // LORA_W)),
            col_spec(0), col_spec(0), col_spec(0), col_spec(0), col_spec(0), col_spec(0), col_spec(0),
            pl.BlockSpec((LORA_PAD, gw), lambda bi, ti, gi: (0, gi)),
            pl.BlockSpec((LORA_PAD, gw), lambda bi, ti, gi: (0, gi)),
            pl.BlockSpec((GATE_LORA, gw), lambda bi, ti, gi: (0, gi)),
            const_spec(masks), const_spec(seg2), const_spec(tri2),
        ],
        out_specs=pl.BlockSpec((1, tt, gw), lambda bi, ti, gi: (bi, ti, gi)),
        out_shape=jax.ShapeDtypeStruct((b, t, RWKV_W), BF16),
        scratch_shapes=[
            pltpu.VMEM((ng * pairs * 3, SUBLANES, PAIR_W), F32),
            pltpu.VMEM((SUBLANES, LORA_W), F32),
            pltpu.VMEM((tt, LORA_W), BF16),
            pltpu.VMEM((ng * pairs, PAIR_W, PAIR_W), F32),
            pltpu.VMEM((pairs, 4, tt, PAIR_W), F32),
        ],
        compiler_params=_cparams(("arbitrary", "arbitrary", "arbitrary")),
        name="rwkv7_mixer",
    )(p3, p3, p3, p3, row(mu), row(mu), row(mu), row(mu),
      row(w0), row(a0), row(k_k), row(k_a), row(r_k), row(ln_g), row(ln_b),
      w2p.astype(BF16), a2p.astype(BF16), g2.astype(BF16), masks, seg2, tri2)


def _lru_kernel(x_ref, gt_ref, cw_ref, cb_ref, wr_ref, br_ref, wi_ref, bi_ref, lam_ref, ng_ref,
                y_ref, xtail_ref, h_ref, a_ref, b_ref, *, tt):
    t = pl.program_id(1)
    first = t == 0
    sp = _softplus(-lam_ref[...])
    rowt = lax.broadcasted_iota(jnp.int32, (tt, LRU_BLOCK), 0)
    sub = rowt & (SUBLANES - 1)
    is_start = rowt == jnp.where(first, 0, -1)

    @pl.when(first)
    def _():
        xtail_ref[...] = jnp.zeros_like(xtail_ref)
        h_ref[...] = jnp.zeros_like(h_ref)

    for h in range(LRU_BLOCKS):
        ls = slice(h * LRU_BLOCK, (h + 1) * LRU_BLOCK)
        x = x_ref[0, :, ls]
        prev = xtail_ref[:, ls]
        xc = cb_ref[:, ls] + x * cw_ref[CONV_WIDTH - 1:CONV_WIDTH, ls]
        for j in range(CONV_WIDTH - 1):
            xc = xc + _shift_rows(x, prev, CONV_WIDTH - 1 - j) * cw_ref[j:j + 1, ls]
        xtail_ref[:, ls] = x[tt - SUBLANES:]
        xh, xl = _split(xc)
        x2 = jnp.concatenate([xh, xl], axis=1)
        rg = jax.nn.sigmoid(_dot(x2, wr_ref[h]) + br_ref[:, ls])
        ig = jax.nn.sigmoid(_dot(x2, wi_ref[h]) + bi_ref[:, ls])
        log_a = -LRU_C * rg * sp[:, ls]
        a = jnp.exp(log_a)
        th = jnp.tanh(log_a)
        mult = jnp.sqrt(-2.0 * th / (1.0 - th))
        mult = jnp.where(is_start, 1.0, mult)
        bx = mult * ig * xc
        for s in (1, 2, 4):
            keep = sub >= s
            a_s = jnp.where(keep, pltpu.roll(a, s, 0), 1.0)
            b_s = jnp.where(keep, pltpu.roll(bx, s, 0), 0.0)
            bx = a * b_s + bx
            a = a * a_s
        a_ref[:, ls] = a
        b_ref[:, ls] = bx

    def group_step(i, hprev):
        off = pl.multiple_of(i * SUBLANES, SUBLANES)
        hcur = a_ref[pl.ds(off, SUBLANES), :] * hprev + b_ref[pl.ds(off, SUBLANES), :]
        b_ref[pl.ds(off, SUBLANES), :] = hcur
        return jnp.broadcast_to(hcur[SUBLANES - 1:SUBLANES], hcur.shape)

    h_ref[...] = lax.fori_loop(0, tt // SUBLANES, group_step, h_ref[...])

    y = b_ref[...] * jax.nn.gelu(gt_ref[0])
    ms = jnp.mean(y * y, axis=-1, keepdims=True)
    y_ref[0] = (y * lax.rsqrt(ms + NORM_EPS) * ng_ref[...]).astype(y_ref.dtype)


def _lru_mixer(p3, conv_w, conv_b, wr, br, wi, bi, lam, norm_g, tt):
    b, t, _ = p3.shape
    row = lambda a: a.reshape(1, -1)
    full = lambda shape: pl.BlockSpec(shape, lambda bi_, ti: (0,) * len(shape))
    twice = lambda w: jnp.concatenate([w, w], axis=1).astype(BF16)
    return pl.pallas_call(
        functools.partial(_lru_kernel, tt=tt),
        grid=(b, t // tt),
        in_specs=[
            pl.BlockSpec((1, tt, LRU_W), lambda bi_, ti: (bi_, ti, OFF_LRU_X // LRU_W)),
            pl.BlockSpec((1, tt, LRU_W), lambda bi_, ti: (bi_, ti, OFF_LRU_G // LRU_W)),
            full((CONV_WIDTH, LRU_W)), full((1, LRU_W)),
            full((LRU_BLOCKS, 2 * LRU_BLOCK, LRU_BLOCK)), full((1, LRU_W)),
            full((LRU_BLOCKS, 2 * LRU_BLOCK, LRU_BLOCK)), full((1, LRU_W)),
            full((1, LRU_W)), full((1, LRU_W)),
        ],
        out_specs=pl.BlockSpec((1, tt, LRU_W), lambda bi_, ti: (bi_, ti, 0)),
        out_shape=jax.ShapeDtypeStruct((b, t, LRU_W), BF16),
        scratch_shapes=[
            pltpu.VMEM((SUBLANES, LRU_W), F32),
            pltpu.VMEM((SUBLANES, LRU_W), F32),
            pltpu.VMEM((tt, LRU_W), F32),
            pltpu.VMEM((tt, LRU_W), F32),
        ],
        compiler_params=_cparams(("arbitrary", "arbitrary")),
        name="rglru_mixer",
    )(p3, p3, conv_w, row(conv_b), twice(wr), row(br), twice(wi), row(bi), row(lam), row(norm_g))


def _outproj_kernel(ya_ref, yb_ref, wa_ref, wb_ref, x_ref, o_ref):
    o_ref[...] = x_ref[...] + _dot(ya_ref[...], wa_ref[...]) + _dot(yb_ref[...], wb_ref[...])


def _outproj(ya, yb, w_bf16, x2d, tm, tn):
    m, d = x2d.shape
    ka = ya.shape[1]
    kb = yb.shape[1]
    return pl.pallas_call(
        _outproj_kernel,
        grid=(m // tm, d // tn),
        in_specs=[
            pl.BlockSpec((tm, ka), lambda i, j: (i, 0)),
            pl.BlockSpec((tm, kb), lambda i, j: (i, 0)),
            pl.BlockSpec((ka, tn), lambda i, j: (0, j)),
            pl.BlockSpec((kb, tn), lambda i, j: (ka // kb, j)),
            pl.BlockSpec((tm, tn), lambda i, j: (i, j)),
        ],
        out_specs=pl.BlockSpec((tm, tn), lambda i, j: (i, j)),
        out_shape=jax.ShapeDtypeStruct((m, d), F32),
        compiler_params=_cparams(("arbitrary", "arbitrary")),
        name="outproj",
    )(ya, yb, w_bf16, w_bf16, x2d)


def _ffn_up_kernel(h_ref, g_ref, wg_ref, wu_ref, o_ref, u_ref):
    @pl.when(pl.program_id(1) == 0)
    def _():
        x = h_ref[...]
        ms = jnp.mean(x * x, axis=-1, keepdims=True)
        u_ref[...] = (x * lax.rsqrt(ms + NORM_EPS) * g_ref[...]).astype(BF16)

    u = u_ref[...]
    gate = _dot(u, wg_ref[...])
    up = _dot(u, wu_ref[...])
    o_ref[...] = (jax.nn.silu(gate) * up).astype(o_ref.dtype)


def _ffn_up(h2d, g, wg_bf16, wu_bf16, tm, tn):
    m, d = h2d.shape
    f = wg_bf16.shape[1]
    return pl.pallas_call(
        _ffn_up_kernel,
        grid=(m // tm, f // tn),
        in_specs=[
            pl.BlockSpec((tm, d), lambda i, j: (i, 0), pipeline_mode=pl.Buffered(1)),
            pl.BlockSpec((1, d), lambda i, j: (0, 0)),
            pl.BlockSpec((d, tn), lambda i, j: (0, j)),
            pl.BlockSpec((d, tn), lambda i, j: (0, j)),
        ],
        out_specs=pl.BlockSpec((tm, tn), lambda i, j: (i, j)),
        out_shape=jax.ShapeDtypeStruct((m, f), BF16),
        scratch_shapes=[pltpu.VMEM((tm, d), BF16)],
        compiler_params=_cparams(("arbitrary", "arbitrary")),
        name="ffn_up",
    )(h2d, g.reshape(1, d), wg_bf16, wu_bf16)


def _ffn_down_kernel(a_ref, w_ref, at_ref, wt_ref, h_ref, g_ref, o_ref, *, final_norm):
    kk = pl.program_id(1)

    @pl.when(kk == 0)
    def _():
        o_ref[...] = h_ref[...] + _dot(at_ref[...], wt_ref[...])

    o_ref[...] += _dot(a_ref[...], w_ref[...])

    if final_norm:
        @pl.when(kk == pl.num_programs(1) - 1)
        def _():
            x = o_ref[...]
            ms = jnp.mean(x * x, axis=-1, keepdims=True)
            o_ref[...] = x * lax.rsqrt(ms + NORM_EPS) * g_ref[...]


def _ffn_down(hid, w_bf16, h2d, final_g, tm, tk):
    m, f = hid.shape
    d = w_bf16.shape[1]
    final_norm = final_g is not None
    g = final_g if final_norm else jnp.ones((d,), F32)
    n_main = f // tk
    tail = f - n_main * tk
    assert tail > 0 and tail % LANES == 0 and (n_main * tk) % tail == 0
    tail_blk = n_main * tk // tail
    return pl.pallas_call(
        functools.partial(_ffn_down_kernel, final_norm=final_norm),
        grid=(m // tm, n_main),
        in_specs=[
            pl.BlockSpec((tm, tk), lambda i, kk: (i, kk)),
            pl.BlockSpec((tk, d), lambda i, kk: (kk, 0)),
            pl.BlockSpec((tm, tail), lambda i, kk: (i, tail_blk)),
            pl.BlockSpec((tail, d), lambda i, kk: (tail_blk, 0)),
            pl.BlockSpec((tm, d), lambda i, kk: (i, 0)),
            pl.BlockSpec((1, d), lambda i, kk: (0, 0)),
        ],
        out_specs=pl.BlockSpec((tm, d), lambda i, kk: (i, 0)),
        out_shape=jax.ShapeDtypeStruct((m, d), F32),
        compiler_params=_cparams(("arbitrary", "arbitrary")),
        name="ffn_down",
    )(hid, w_bf16, hid, w_bf16, h2d, g.reshape(1, d))


def _pad_rows(w, rows):
    return jnp.pad(w, ((0, rows - w.shape[0]), (0, 0)))


def _layer(h2d, bsz, seq, final_g, norm_mix_g, w_in, mu_shift, rwkv_w0, rwkv_w2, rwkv_a0, rwkv_a2,
           rwkv_g2, rwkv_k_k, rwkv_k_a, rwkv_r_k, rwkv_ln_g, rwkv_ln_b, conv_w, conv_b,
           lru_wr, lru_br, lru_wi, lru_bi, lru_lambda, lru_norm_g, w_out,
           norm_ffn_g, ffn_w_gate, ffn_w_up, ffn_w_down):
    d = h2d.shape[1]
    o1, o2 = 3 * RWKV_W, 3 * RWKV_W + DECAY_LORA
    o3 = o2 + AAA_LORA
    o4 = o3 + GATE_LORA

    def relayout(a, fill):
        padw = [(0, 0)] * (a.ndim - 1) + [(0, LORA_PAD - DECAY_LORA)]
        wd = jnp.pad(a[..., o1:o2], padw, constant_values=fill)
        ad = jnp.pad(a[..., o2:o3], padw, constant_values=fill)
        return jnp.concatenate([a[..., o4:], a[..., :o1], wd, ad, a[..., o3:o4]], axis=-1)

    w_in_p = relayout(w_in.astype(BF16), 0.0)
    mu_full = jnp.concatenate([mu_shift, jnp.zeros((2 * LRU_W,), F32)])
    mu_p = relayout(mu_full, 0.0)
    w2p = _pad_rows(rwkv_w2, LORA_PAD)
    a2p = _pad_rows(rwkv_a2, LORA_PAD)

    p = _norm_matmul(h2d, norm_mix_g, w_in_p, tm=1024, tn=512)
    p3 = p.reshape(bsz, seq, P_COLS)
    y_a = _rwkv_mixer(p3, mu_p, rwkv_w0, w2p, rwkv_a0, a2p, rwkv_g2, rwkv_k_k, rwkv_k_a,
                      rwkv_r_k.reshape(-1), rwkv_ln_g, rwkv_ln_b, tt=512, pairs=4)
    y_b = _lru_mixer(p3, conv_w, conv_b, lru_wr, lru_br, lru_wi, lru_bi, lru_lambda, lru_norm_g, tt=256)
    h1 = _outproj(y_a.reshape(-1, RWKV_W), y_b.reshape(-1, LRU_W), w_out.astype(BF16), h2d, tm=512, tn=1024)
    hid = _ffn_up(h1, norm_ffn_g, ffn_w_gate.astype(BF16), ffn_w_up.astype(BF16), tm=1024, tn=256)
    return _ffn_down(hid, ffn_w_down.astype(BF16), h1, final_g, tm=512, tk=512)


def kernel(x, norm_mix_g, w_in, mu_shift, rwkv_w0, rwkv_w2, rwkv_a0, rwkv_a2, rwkv_g2, rwkv_k_k, rwkv_k_a, rwkv_r_k, rwkv_ln_g, rwkv_ln_b, conv_w, conv_b, lru_wr, lru_br, lru_wi, lru_bi, lru_lambda, lru_norm_g, w_out, norm_ffn_g, ffn_w_gate, ffn_w_up, ffn_w_down, norm_final_g):
    bsz, seq, d = x.shape
    h = x.astype(F32).reshape(bsz * seq, d)
    depth = w_in.shape[0]
    for l in range(depth):
        last = l == depth - 1
        h = _layer(h, bsz, seq, norm_final_g if last else None, norm_mix_g[l], w_in[l], mu_shift[l], rwkv_w0[l], rwkv_w2[l], rwkv_a0[l],
                   rwkv_a2[l], rwkv_g2[l], rwkv_k_k[l], rwkv_k_a[l], rwkv_r_k[l], rwkv_ln_g[l],
                   rwkv_ln_b[l], conv_w[l], conv_b[l], lru_wr[l], lru_br[l], lru_wi[l], lru_bi[l],
                   lru_lambda[l], lru_norm_g[l], w_out[l], norm_ffn_g[l], ffn_w_gate[l], ffn_w_up[l],
                   ffn_w_down[l])
    return h.reshape(bsz, seq, d).astype(x.dtype)
```

```python
import functools

import jax
import jax.numpy as jnp
import numpy as np
from jax import lax
from jax.experimental import pallas as pl
from jax.experimental.pallas import tpu as pltpu

F32 = jnp.float32
BF16 = jnp.bfloat16

D_MODEL = 4096
RWKV_W = 2048
HEAD = 64
DECAY_LORA = 96
AAA_LORA = 96
GATE_LORA = 256
LRU_W = 2048
LRU_BLOCK = 128
LRU_BLOCKS = LRU_W // LRU_BLOCK
CONV_WIDTH = 4
LRU_C = 8.0
NORM_EPS = 1e-6
GN_EPS = 64e-5

LANES = 128
SUBLANES = 8
LORA_PAD = 128
LORA_W = 2 * LORA_PAD + GATE_LORA
OFF_LRU_X = 0
OFF_LRU_G = LRU_W
OFF_R = 2 * LRU_W
OFF_K = OFF_R + RWKV_W
OFF_V = OFF_K + RWKV_W
OFF_LORA = OFF_V + RWKV_W
P_COLS = OFF_LORA + LORA_W

VMEM_LIMIT = 56 * 1024 * 1024

CHUNK = 64
CHUNK_SHIFT = CHUNK.bit_length() - 1
PAIR_W = 2 * HEAD
STREAM_SKEW = 6


def _cparams(sem):
    return pltpu.CompilerParams(dimension_semantics=sem, vmem_limit_bytes=VMEM_LIMIT)


def _dot(a, b):
    return jnp.dot(a, b, preferred_element_type=F32)


def _dot_nt(a, b):
    return lax.dot_general(a, b, (((1,), (1,)), ((), ())), preferred_element_type=F32)


def _split(x):
    hi = x.astype(BF16)
    lo = (x - hi.astype(F32)).astype(BF16)
    return hi, lo


def _dot_hilo(a, b2):
    ah, al = _split(a)
    return _dot(jnp.concatenate([ah, al], axis=1), b2)


def _softplus(x):
    return jnp.maximum(x, 0.0) + jnp.log1p(jnp.exp(-jnp.abs(x)))


def _shift_rows(x, prev_tail, s):
    rolled = pltpu.roll(x, s, 0)
    fix = pltpu.roll(prev_tail, s, 0)
    row = lax.broadcasted_iota(jnp.int32, (SUBLANES, x.shape[1]), 0)
    head = jnp.where(row < s, fix, rolled[:SUBLANES])
    return jnp.concatenate([head, rolled[SUBLANES:]], axis=0)


def _norm_matmul_kernel(x_ref, g_ref, w_ref, o_ref, u_ref):
    @pl.when(pl.program_id(1) == 0)
    def _():
        x = x_ref[...]
        ms = jnp.mean(x * x, axis=-1, keepdims=True)
        u_ref[...] = (x * lax.rsqrt(ms + NORM_EPS) * g_ref[...]).astype(BF16)

    o_ref[...] = _dot(u_ref[...], w_ref[...]).astype(o_ref.dtype)


def _norm_matmul(x2d, g, w_bf16, tm, tn):
    m, d = x2d.shape
    n = w_bf16.shape[1]
    return pl.pallas_call(
        _norm_matmul_kernel,
        grid=(m // tm, n // tn),
        in_specs=[
            pl.BlockSpec((tm, d), lambda i, j: (i, 0), pipeline_mode=pl.Buffered(1)),
            pl.BlockSpec((1, d), lambda i, j: (0, 0)),
            pl.BlockSpec((d, tn), lambda i, j: (0, j)),
        ],
        out_specs=pl.BlockSpec((tm, tn), lambda i, j: (i, j)),
        out_shape=jax.ShapeDtypeStruct((m, n), F32),
        scratch_shapes=[pltpu.VMEM((tm, d), BF16)],
        compiler_params=_cparams(("arbitrary", "arbitrary")),
        name="norm_inproj",
    )(x2d, g.reshape(1, d), w_bf16)


def _rwkv_kernel(r_ref, k_ref, v_ref, lo_ref,
                 mur_ref, muk_ref, muv_ref, mulo_ref,
                 w0_ref, a0_ref, kk_ref, ka_ref, rk_ref, lng_ref, lnb_ref,
                 w2_ref, a2_ref, g2_ref, mask_ref, seg_ref, tri_ref,
                 y_ref,
                 tail_ref, lotail_ref, loact_ref, s_ref, yraw_ref,
                 *, tt, pairs):
    t = pl.program_id(1)
    g = pl.program_id(2)
    nc = tt // CHUNK
    first = t == 0

    @pl.when(first)
    def _():
        for i in range(pairs * 3):
            tail_ref[g * pairs * 3 + i] = jnp.zeros((SUBLANES, PAIR_W), F32)

    @pl.when(first & (g == 0))
    def _():
        lotail_ref[...] = jnp.zeros_like(lotail_ref)

    @pl.when(g == 0)
    def _():
        lo = lo_ref[0]
        lo_prev = _shift_rows(lo, lotail_ref[...], 1)
        lotail_ref[...] = lo[tt - SUBLANES:]
        xs = lo + (lo_prev - lo) * mulo_ref[...]
        loact_ref[:, 0:LORA_PAD] = jnp.tanh(xs[:, 0:LORA_PAD]).astype(BF16)
        loact_ref[:, LORA_PAD:2 * LORA_PAD] = xs[:, LORA_PAD:2 * LORA_PAD].astype(BF16)
        loact_ref[:, 2 * LORA_PAD:] = jax.nn.sigmoid(xs[:, 2 * LORA_PAD:]).astype(BF16)

    lane = lax.broadcasted_iota(jnp.int32, (1, PAIR_W), 1)
    m0 = lane < HEAD
    strict2 = mask_ref[:, 0:2 * PAIR_W] > 0.5
    incl2 = mask_ref[:, 2 * PAIR_W:4 * PAIR_W] > 0.5
    eye = mask_ref[:, 4 * PAIR_W:]
    tri2 = tri_ref[...]

    def stack(xc):
        return jnp.concatenate([jnp.where(m0, xc, 0.0), jnp.where(m0, 0.0, xc)], axis=0)

    def seg_sum(x):
        return _dot_hilo(x, seg_ref[...])

    def seg_sum_1pass(x):
        return _dot(x.astype(BF16), seg_ref[0:PAIR_W])

    wdt = loact_ref[:, 0:LORA_PAD]
    adx = loact_ref[:, LORA_PAD:2 * LORA_PAD]
    gds = loact_ref[:, 2 * LORA_PAD:]

    def pair_stream(p):
        bodies = []
        ls = slice(p * PAIR_W, (p + 1) * PAIR_W)
        slot = g * pairs + p

        def shifted(ref, mu_ref, idx):
            x = ref[0, :, ls]
            xp = _shift_rows(x, tail_ref[slot * 3 + idx], 1)
            tail_ref[slot * 3 + idx] = x[tt - SUBLANES:]
            return x + (xp - x) * mu_ref[:, ls]

        r = shifted(r_ref, mur_ref, 0)
        k = shifted(k_ref, muk_ref, 1)
        v = shifted(v_ref, muv_ref, 2)

        wlog = -_softplus(-(w0_ref[:, ls] + _dot(wdt, w2_ref[:, ls]))) - 0.5
        lw = -jnp.exp(wlog)
        asig = jax.nn.sigmoid(a0_ref[:, ls] + _dot(adx, a2_ref[:, ls]))
        gate = _dot(gds, g2_ref[:, ls])

        kk = k * kk_ref[:, ls]
        kk = kk * lax.rsqrt(jnp.maximum(seg_sum(kk * kk), 1e-24))
        k = k * (1.0 + (asig - 1.0) * ka_ref[:, ls])
        a_in = -kk
        b_in = kk * asig
        yraw_ref[p, 1] = r * k * rk_ref[:, ls]
        yraw_ref[p, 2] = v
        yraw_ref[p, 3] = gate
        yield

        for c in range(nc):
            cs = slice(c * CHUNK, (c + 1) * CHUNK)
            lwc = lw[cs]
            lwh, lwl = _split(lwc)
            cum = _dot(tri2, jnp.concatenate([lwh, lwl], axis=0))
            cl = cum[CHUNK - 1:CHUNK]
            e_n = jnp.exp(-cum)
            e_l = jnp.exp(cl - cum)
            body = dict(
                at=stack(a_in[cs] * jnp.exp(cum - lwc)).astype(BF16),
                rt=stack(r[cs] * jnp.exp(cum)),
                bk=jnp.concatenate([stack(b_in[cs] * e_n), stack(k[cs] * e_n)], axis=0).astype(BF16),
                bkl=jnp.concatenate([stack(b_in[cs] * e_l), stack(k[cs] * e_l)], axis=0).astype(BF16),
                vs=stack(v[cs]),
                el=jnp.exp(cl),
            )
            bodies.append(body)
            yield

        for bd in bodies:
            sa = jnp.where(strict2, _dot_nt(bd["at"], bd["bk"]), 0.0)
            sr = jnp.where(incl2, _dot_nt(bd["rt"].astype(BF16), bd["bk"]), 0.0)
            bd["n"] = sa[:, :PAIR_W]
            bd["aak"] = sa[:, PAIR_W:].astype(BF16)
            bd["arb"] = sr[:, :PAIR_W].astype(BF16)
            bd["ark"] = sr[:, PAIR_W:].astype(BF16)
            bd["vsb"] = bd["vs"].astype(BF16)
        yield
        for bd in bodies:
            nb = bd["n"].astype(BF16)
            bd["t"] = eye + bd["n"]
            bd["nb"] = _dot(nb, nb).astype(BF16)
        yield
        for _ in range(CHUNK_SHIFT - 2):
            for bd in bodies:
                nb = bd["nb"]
                res = _dot(nb, jnp.concatenate([nb, bd["t"].astype(BF16)], axis=1))
                bd["nb"] = res[:, :PAIR_W].astype(BF16)
                bd["t"] = bd["t"] + res[:, PAIR_W:]
            yield
        for bd in bodies:
            bd["t"] = bd["t"] + _dot(bd["nb"], bd["t"].astype(BF16))
            bd["x0"] = _dot(bd["aak"], bd["vsb"])
        yield
        for bd in bodies:
            bd["wu"] = _dot(bd["t"].astype(BF16), jnp.concatenate([bd["at"], bd["x0"].astype(BF16)], axis=1))
        yield
        for bd in bodies:
            qy = _dot(bd["arb"], bd["wu"].astype(BF16))
            bd["qp"] = (bd["rt"] + qy[:, :PAIR_W]).astype(BF16)
            bd["y0"] = qy[:, PAIR_W:] + _dot(bd["ark"], bd["vsb"])
        yield
        for bd in bodies:
            wut = bd["wu"].T
            bd["mx"] = _dot(wut[:PAIR_W].astype(BF16), bd["bkl"][:PAIR_W]).astype(BF16)
            uv_t = jnp.concatenate([wut[PAIR_W:], bd["vs"].T], axis=1).astype(BF16)
            bd["g0"] = _dot(uv_t, bd["bkl"])
        yield

        s = s_ref[slot]
        for bd in bodies:
            bd["s_in"] = s.astype(BF16)
            s = s * bd["el"] + _dot(bd["s_in"], bd["mx"]) + bd["g0"]
            yield
        s_ref[slot] = s
        for c, bd in enumerate(bodies):
            y2 = _dot_nt(bd["qp"], bd["s_in"]) + bd["y0"]
            yraw_ref[p, 0, c * CHUNK:(c + 1) * CHUNK, :] = y2[:CHUNK] + y2[CHUNK:]
        yield

        y = yraw_ref[p, 0]
        mean = seg_sum_1pass(y) * (1.0 / HEAD)
        yc = y - mean
        var = seg_sum(yc * yc) * (1.0 / HEAD)
        yn = yc * lax.rsqrt(var + GN_EPS) * lng_ref[:, ls] + lnb_ref[:, ls]
        bonus = seg_sum_1pass(yraw_ref[p, 1])
        out = (yn + bonus * yraw_ref[p, 2]) * yraw_ref[p, 3]
        y_ref[0, :, ls] = out.astype(y_ref.dtype)
        yield

    @pl.when(first)
    def _():
        for p in range(pairs):
            s_ref[g * pairs + p] = jnp.zeros((PAIR_W, PAIR_W), F32)

    pending = [pair_stream(p) for p in range(pairs)]
    live = []
    tick = 0
    while pending or live:
        if pending and tick % STREAM_SKEW == 0:
            live.append(pending.pop(0))
        for stream in list(live):
            if next(stream, "done") == "done":
                live.remove(stream)
        tick += 1


def _rwkv_constants():
    idx = np.arange(PAIR_W)
    same = (idx[:, None] // CHUNK) == (idx[None, :] // CHUNK)
    strict = same & (idx[None, :] < idx[:, None])
    incl = same & (idx[None, :] <= idx[:, None])
    masks = np.concatenate([strict, strict, incl, incl, np.eye(PAIR_W, dtype=bool)], axis=1)
    seg2 = np.concatenate([same, same], axis=0)
    tr = np.arange(CHUNK)
    tri = tr[None, :] <= tr[:, None]
    tri2 = np.concatenate([tri, tri], axis=1)
    return (jnp.asarray(masks, F32), jnp.asarray(seg2, BF16), jnp.asarray(tri2, BF16))


def _rwkv_mixer(p3, mu, w0, w2p, a0, a2p, g2, k_k, k_a, r_k, ln_g, ln_b, tt, pairs):
    b, t, _ = p3.shape
    gw = pairs * PAIR_W
    ng = RWKV_W // gw
    row = lambda a: a.reshape(1, -1)
    col_spec = lambda off: pl.BlockSpec((1, gw), lambda bi, ti, gi, off=off: (0, off // gw + gi))
    act_spec = lambda off: pl.BlockSpec((1, tt, gw), lambda bi, ti, gi, off=off: (bi, ti, off // gw + gi))
    const_spec = lambda a: pl.BlockSpec(a.shape, lambda bi, ti, gi: (0, 0))
    kern = functools.partial(_rwkv_kernel, tt=tt, pairs=pairs)
    masks, seg2, tri2 = _rwkv_constants()
    return pl.pallas_call(
        kern,
        grid=(b, t // tt, ng),
        in_specs=[
            act_spec(OFF_R), act_spec(OFF_K), act_spec(OFF_V),
            pl.BlockSpec((1, tt, LORA_W), lambda bi, ti, gi: (bi, ti, OFF_LORA // LORA_W)),
            col_spec(OFF_R), col_spec(OFF_K), col_spec(OFF_V),
            pl.BlockSpec((1, LORA_W), lambda bi, ti, gi: (0, OFF_LORA // LORA_W)),
            col_spec(0), col_spec(0), col_spec(0), col_spec(0), col_spec(0), col_spec(0), col_spec(0),
            pl.BlockSpec((LORA_PAD, gw), lambda bi, ti, gi: (0, gi)),
            pl.BlockSpec((LORA_PAD, gw), lambda bi, ti, gi: (0, gi)),
            pl.BlockSpec((GATE_LORA, gw), lambda bi, ti, gi: (0, gi)),
            const_spec(masks), const_spec(seg2), const_spec(tri2),
        ],
        out_specs=pl.BlockSpec((1, tt, gw), lambda bi, ti, gi: (bi, ti, gi)),
        out_shape=jax.ShapeDtypeStruct((b, t, RWKV_W), BF16),
        scratch_shapes=[
            pltpu.VMEM((ng * pairs * 3, SUBLANES, PAIR_W), F32),
            pltpu.VMEM((SUBLANES, LORA_W), F32),
            pltpu.VMEM((tt, LORA_W), BF16),
            pltpu.VMEM((ng * pairs, PAIR_W, PAIR_W), F32),
            pltpu.VMEM((pairs, 4, tt, PAIR_W), F32),
        ],
        compiler_params=_cparams(("arbitrary", "arbitrary", "arbitrary")),
        name="rwkv7_mixer",
    )(p3, p3, p3, p3, row(mu), row(mu), row(mu), row(mu),
      row(w0), row(a0), row(k_k), row(k_a), row(r_k), row(ln_g), row(ln_b),
      w2p.astype(BF16), a2p.astype(BF16), g2.astype(BF16), masks, seg2, tri2)


def _lru_kernel(x_ref, gt_ref, cw_ref, cb_ref, wr_ref, br_ref, wi_ref, bi_ref, lam_ref, ng_ref,
                y_ref, xtail_ref, h_ref, a_ref, b_ref, *, tt):
    t = pl.program_id(1)
    first = t == 0
    sp = _softplus(-lam_ref[...])
    rowt = lax.broadcasted_iota(jnp.int32, (tt, LRU_BLOCK), 0)
    sub = rowt & (SUBLANES - 1)
    keeps = [(s, sub >= s) for s in (1, 2, 4)]
    is_start = rowt == jnp.where(first, 0, -1)

    @pl.when(first)
    def _():
        xtail_ref[...] = jnp.zeros_like(xtail_ref)
        h_ref[...] = jnp.zeros_like(h_ref)

    for h in range(LRU_BLOCKS):
        ls = slice(h * LRU_BLOCK, (h + 1) * LRU_BLOCK)
        x = x_ref[0, :, ls]
        prev = xtail_ref[:, ls]
        xc = cb_ref[:, ls] + x * cw_ref[CONV_WIDTH - 1:CONV_WIDTH, ls]
        for j in range(CONV_WIDTH - 1):
            xc = xc + _shift_rows(x, prev, CONV_WIDTH - 1 - j) * cw_ref[j:j + 1, ls]
        xtail_ref[:, ls] = x[tt - SUBLANES:]
        xh, xl = _split(xc)
        x2 = jnp.concatenate([xh, xl], axis=1)
        rg = jax.nn.sigmoid(_dot(x2, wr_ref[h]) + br_ref[:, ls])
        ig = jax.nn.sigmoid(_dot(x2, wi_ref[h]) + bi_ref[:, ls])
        log_a = -LRU_C * rg * sp[:, ls]
        a = jnp.exp(log_a)
        th = jnp.tanh(log_a)
        mult = jnp.sqrt(-2.0 * th / (1.0 - th))
        mult = jnp.where(is_start, 1.0, mult)
        bx = mult * ig * xc
        for s, keep in keeps:
            a_s = jnp.where(keep, pltpu.roll(a, s, 0), 1.0)
            b_s = jnp.where(keep, pltpu.roll(bx, s, 0), 0.0)
            bx = a * b_s + bx
            a = a * a_s
        a_ref[:, ls] = a
        b_ref[:, ls] = bx

    def group_step(i, hprev):
        off = pl.multiple_of(i * SUBLANES, SUBLANES)
        hcur = a_ref[pl.ds(off, SUBLANES), :] * hprev + b_ref[pl.ds(off, SUBLANES), :]
        b_ref[pl.ds(off, SUBLANES), :] = hcur
        return jnp.broadcast_to(hcur[SUBLANES - 1:SUBLANES], hcur.shape)

    h_ref[...] = lax.fori_loop(0, tt // SUBLANES, group_step, h_ref[...])

    y = b_ref[...] * jax.nn.gelu(gt_ref[0])
    ms = jnp.mean(y * y, axis=-1, keepdims=True)
    y_ref[0] = (y * lax.rsqrt(ms + NORM_EPS) * ng_ref[...]).astype(y_ref.dtype)


def _lru_mixer(p3, conv_w, conv_b, wr, br, wi, bi, lam, norm_g, tt):
    b, t, _ = p3.shape
    row = lambda a: a.reshape(1, -1)
    full = lambda shape: pl.BlockSpec(shape, lambda bi_, ti: (0,) * len(shape))
    twice = lambda w: jnp.concatenate([w, w], axis=1).astype(BF16)
    return pl.pallas_call(
        functools.partial(_lru_kernel, tt=tt),
        grid=(b, t // tt),
        in_specs=[
            pl.BlockSpec((1, tt, LRU_W), lambda bi_, ti: (bi_, ti, OFF_LRU_X // LRU_W)),
            pl.BlockSpec((1, tt, LRU_W), lambda bi_, ti: (bi_, ti, OFF_LRU_G // LRU_W)),
            full((CONV_WIDTH, LRU_W)), full((1, LRU_W)),
            full((LRU_BLOCKS, 2 * LRU_BLOCK, LRU_BLOCK)), full((1, LRU_W)),
            full((LRU_BLOCKS, 2 * LRU_BLOCK, LRU_BLOCK)), full((1, LRU_W)),
            full((1, LRU_W)), full((1, LRU_W)),
        ],
        out_specs=pl.BlockSpec((1, tt, LRU_W), lambda bi_, ti: (bi_, ti, 0)),
        out_shape=jax.ShapeDtypeStruct((b, t, LRU_W), BF16),
        scratch_shapes=[
            pltpu.VMEM((SUBLANES, LRU_W), F32),
            pltpu.VMEM((SUBLANES, LRU_W), F32),
            pltpu.VMEM((tt, LRU_W), F32),
            pltpu.VMEM((tt, LRU_W), F32),
        ],
        compiler_params=_cparams(("arbitrary", "arbitrary")),
        name="rglru_mixer",
    )(p3, p3, conv_w, row(conv_b), twice(wr), row(br), twice(wi), row(bi), row(lam), row(norm_g))


def _outproj_kernel(ya_ref, yb_ref, wa_ref, wb_ref, x_ref, o_ref):
    o_ref[...] = x_ref[...] + _dot(ya_ref[...], wa_ref[...]) + _dot(yb_ref[...], wb_ref[...])


def _outproj(ya, yb, w_bf16, x2d, tm, tn):
    m, d = x2d.shape
    ka = ya.shape[1]
    kb = yb.shape[1]
    return pl.pallas_call(
        _outproj_kernel,
        grid=(m // tm, d // tn),
        in_specs=[
            pl.BlockSpec((tm, ka), lambda i, j: (i, 0)),
            pl.BlockSpec((tm, kb), lambda i, j: (i, 0)),
            pl.BlockSpec((ka, tn), lambda i, j: (0, j)),
            pl.BlockSpec((kb, tn), lambda i, j: (ka // kb, j)),
            pl.BlockSpec((tm, tn), lambda i, j: (i, j)),
        ],
        out_specs=pl.BlockSpec((tm, tn), lambda i, j: (i, j)),
        out_shape=jax.ShapeDtypeStruct((m, d), F32),
        compiler_params=_cparams(("arbitrary", "arbitrary")),
        name="outproj",
    )(ya, yb, w_bf16, w_bf16, x2d)


def _ffn_up_kernel(h_ref, g_ref, wg_ref, wu_ref, o_ref, u_ref):
    @pl.when(pl.program_id(1) == 0)
    def _():
        x = h_ref[...]
        ms = jnp.mean(x * x, axis=-1, keepdims=True)
        u_ref[...] = (x * lax.rsqrt(ms + NORM_EPS) * g_ref[...]).astype(BF16)

    u = u_ref[...]
    gate = _dot(u, wg_ref[...])
    up = _dot(u, wu_ref[...])
    o_ref[...] = (jax.nn.silu(gate) * up).astype(o_ref.dtype)


def _ffn_up(h2d, g, wg_bf16, wu_bf16, tm, tn):
    m, d = h2d.shape
    f = wg_bf16.shape[1]
    return pl.pallas_call(
        _ffn_up_kernel,
        grid=(m // tm, f // tn),
        in_specs=[
            pl.BlockSpec((tm, d), lambda i, j: (i, 0), pipeline_mode=pl.Buffered(1)),
            pl.BlockSpec((1, d), lambda i, j: (0, 0)),
            pl.BlockSpec((d, tn), lambda i, j: (0, j)),
            pl.BlockSpec((d, tn), lambda i, j: (0, j)),
        ],
        out_specs=pl.BlockSpec((tm, tn), lambda i, j: (i, j)),
        out_shape=jax.ShapeDtypeStruct((m, f), BF16),
        scratch_shapes=[pltpu.VMEM((tm, d), BF16)],
        compiler_params=_cparams(("arbitrary", "arbitrary")),
        name="ffn_up",
    )(h2d, g.reshape(1, d), wg_bf16, wu_bf16)


def _ffn_down_kernel(a_ref, w_ref, at_ref, wt_ref, h_ref, g_ref, o_ref, *, final_norm):
    kk = pl.program_id(1)

    @pl.when(kk == 0)
    def _():
        o_ref[...] = h_ref[...] + _dot(at_ref[...], wt_ref[...])

    o_ref[...] += _dot(a_ref[...], w_ref[...])

    if final_norm:
        @pl.when(kk == pl.num_programs(1) - 1)
        def _():
            x = o_ref[...]
            ms = jnp.mean(x * x, axis=-1, keepdims=True)
            o_ref[...] = x * lax.rsqrt(ms + NORM_EPS) * g_ref[...]


def _ffn_down(hid, w_bf16, h2d, final_g, tm, tk):
    m, f = hid.shape
    d = w_bf16.shape[1]
    final_norm = final_g is not None
    g = final_g if final_norm else jnp.ones((d,), F32)
    n_main = f // tk
    tail = f - n_main * tk
    assert tail > 0 and tail % LANES == 0 and (n_main * tk) % tail == 0
    tail_blk = n_main * tk // tail
    return pl.pallas_call(
        functools.partial(_ffn_down_kernel, final_norm=final_norm),
        grid=(m // tm, n_main),
        in_specs=[
            pl.BlockSpec((tm, tk), lambda i, kk: (i, kk)),
            pl.BlockSpec((tk, d), lambda i, kk: (kk, 0)),
            pl.BlockSpec((tm, tail), lambda i, kk: (i, tail_blk)),
            pl.BlockSpec((tail, d), lambda i, kk: (tail_blk, 0)),
            pl.BlockSpec((tm, d), lambda i, kk: (i, 0)),
            pl.BlockSpec((1, d), lambda i, kk: (0, 0)),
        ],
        out_specs=pl.BlockSpec((tm, d), lambda i, kk: (i, 0)),
        out_shape=jax.ShapeDtypeStruct((m, d), F32),
        compiler_params=_cparams(("arbitrary", "arbitrary")),
        name="ffn_down",
    )(hid, w_bf16, hid, w_bf16, h2d, g.reshape(1, d))


def _pad_rows(w, rows):
    return jnp.pad(w, ((0, rows - w.shape[0]), (0, 0)))


def _layer(h2d, bsz, seq, final_g, norm_mix_g, w_in, mu_shift, rwkv_w0, rwkv_w2, rwkv_a0, rwkv_a2,
           rwkv_g2, rwkv_k_k, rwkv_k_a, rwkv_r_k, rwkv_ln_g, rwkv_ln_b, conv_w, conv_b,
           lru_wr, lru_br, lru_wi, lru_bi, lru_lambda, lru_norm_g, w_out,
           norm_ffn_g, ffn_w_gate, ffn_w_up, ffn_w_down):
    d = h2d.shape[1]
    o1, o2 = 3 * RWKV_W, 3 * RWKV_W + DECAY_LORA
    o3 = o2 + AAA_LORA
    o4 = o3 + GATE_LORA

    def relayout(a, fill):
        padw = [(0, 0)] * (a.ndim - 1) + [(0, LORA_PAD - DECAY_LORA)]
        wd = jnp.pad(a[..., o1:o2], padw, constant_values=fill)
        ad = jnp.pad(a[..., o2:o3], padw, constant_values=fill)
        return jnp.concatenate([a[..., o4:], a[..., :o1], wd, ad, a[..., o3:o4]], axis=-1)

    w_in_p = relayout(w_in.astype(BF16), 0.0)
    mu_full = jnp.concatenate([mu_shift, jnp.zeros((2 * LRU_W,), F32)])
    mu_p = relayout(mu_full, 0.0)
    w2p = _pad_rows(rwkv_w2, LORA_PAD)
    a2p = _pad_rows(rwkv_a2, LORA_PAD)

    p = _norm_matmul(h2d, norm_mix_g, w_in_p, tm=1024, tn=512)
    p3 = p.reshape(bsz, seq, P_COLS)
    y_a = _rwkv_mixer(p3, mu_p, rwkv_w0, w2p, rwkv_a0, a2p, rwkv_g2, rwkv_k_k, rwkv_k_a,
                      rwkv_r_k.reshape(-1), rwkv_ln_g, rwkv_ln_b, tt=512, pairs=4)
    y_b = _lru_mixer(p3, conv_w, conv_b, lru_wr, lru_br, lru_wi, lru_bi, lru_lambda, lru_norm_g, tt=256)
    h1 = _outproj(y_a.reshape(-1, RWKV_W), y_b.reshape(-1, LRU_W), w_out.astype(BF16), h2d, tm=1024, tn=1024)
    hid = _ffn_up(h1, norm_ffn_g, ffn_w_gate.astype(BF16), ffn_w_up.astype(BF16), tm=1024, tn=256)
    return _ffn_down(hid, ffn_w_down.astype(BF16), h1, final_g, tm=512, tk=768)


def kernel(x, norm_mix_g, w_in, mu_shift, rwkv_w0, rwkv_w2, rwkv_a0, rwkv_a2, rwkv_g2, rwkv_k_k, rwkv_k_a, rwkv_r_k, rwkv_ln_g, rwkv_ln_b, conv_w, conv_b, lru_wr, lru_br, lru_wi, lru_bi, lru_lambda, lru_norm_g, w_out, norm_ffn_g, ffn_w_gate, ffn_w_up, ffn_w_down, norm_final_g):
    bsz, seq, d = x.shape
    h = x.astype(F32).reshape(bsz * seq, d)
    depth = w_in.shape[0]
    for l in range(depth):
        last = l == depth - 1
        h = _layer(h, bsz, seq, norm_final_g if last else None, norm_mix_g[l], w_in[l], mu_shift[l], rwkv_w0[l], rwkv_w2[l], rwkv_a0[l],
                   rwkv_a2[l], rwkv_g2[l], rwkv_k_k[l], rwkv_k_a[l], rwkv_r_k[l], rwkv_ln_g[l],
                   rwkv_ln_b[l], conv_w[l], conv_b[l], lru_wr[l], lru_br[l], lru_wi[l], lru_bi[l],
                   lru_lambda[l], lru_norm_g[l], w_out[l], norm_ffn_g[l], ffn_w_gate[l], ffn_w_up[l],
                   ffn_w_down[l])
    return h.reshape(bsz, seq, d).astype(x.dtype)
```

```python
import functools

import jax
import jax.numpy as jnp
import numpy as np
from jax import lax
from jax.experimental import pallas as pl
from jax.experimental.pallas import tpu as pltpu

F32 = jnp.float32
BF16 = jnp.bfloat16

D_MODEL = 4096
RWKV_W = 2048
HEAD = 64
DECAY_LORA = 96
AAA_LORA = 96
GATE_LORA = 256
LRU_W = 2048
LRU_BLOCK = 128
LRU_BLOCKS = LRU_W // LRU_BLOCK
CONV_WIDTH = 4
LRU_C = 8.0
NORM_EPS = 1e-6
GN_EPS = 64e-5

LANES = 128
SUBLANES = 8
LORA_PAD = 128
LORA_W = 2 * LORA_PAD + GATE_LORA
OFF_LRU_X = 0
OFF_LRU_G = LRU_W
OFF_R = 2 * LRU_W
OFF_K = OFF_R + RWKV_W
OFF_V = OFF_K + RWKV_W
OFF_LORA = OFF_V + RWKV_W
P_COLS = OFF_LORA + LORA_W

VMEM_LIMIT = 56 * 1024 * 1024

CHUNK = 64
CHUNK_SHIFT = CHUNK.bit_length() - 1
PAIR_W = 2 * HEAD
STREAM_SKEW = 6


def _cparams(sem):
    return pltpu.CompilerParams(dimension_semantics=sem, vmem_limit_bytes=VMEM_LIMIT)


def _dot(a, b):
    return jnp.dot(a, b, preferred_element_type=F32)


def _dot_nt(a, b):
    return lax.dot_general(a, b, (((1,), (1,)), ((), ())), preferred_element_type=F32)


def _split(x):
    hi = x.astype(BF16)
    lo = (x - hi.astype(F32)).astype(BF16)
    return hi, lo


def _dot_hilo(a, b2):
    ah, al = _split(a)
    return _dot(jnp.concatenate([ah, al], axis=1), b2)


def _softplus(x):
    return jnp.maximum(x, 0.0) + jnp.log1p(jnp.exp(-jnp.abs(x)))


def _shift_rows(x, prev_tail, s):
    rolled = pltpu.roll(x, s, 0)
    fix = pltpu.roll(prev_tail, s, 0)
    row = lax.broadcasted_iota(jnp.int32, (SUBLANES, x.shape[1]), 0)
    head = jnp.where(row < s, fix, rolled[:SUBLANES])
    return jnp.concatenate([head, rolled[SUBLANES:]], axis=0)


NORM_ROWS = 256


def _rmsnorm_rows(x_ref, g_ref, u_ref):
    for r in range(0, x_ref.shape[0], NORM_ROWS):
        x = x_ref[r:r + NORM_ROWS, :]
        ms = jnp.mean(x * x, axis=-1, keepdims=True)
        u_ref[r:r + NORM_ROWS, :] = (x * lax.rsqrt(ms + NORM_EPS) * g_ref[...]).astype(u_ref.dtype)


def _norm_matmul_kernel(x_ref, g_ref, w_ref, o_ref, u_ref):
    @pl.when(pl.program_id(1) == 0)
    def _():
        _rmsnorm_rows(x_ref, g_ref, u_ref)

    o_ref[...] = _dot(u_ref[...], w_ref[...]).astype(o_ref.dtype)


def _norm_matmul(x2d, g, w_bf16, tm, tn):
    m, d = x2d.shape
    n = w_bf16.shape[1]
    return pl.pallas_call(
        _norm_matmul_kernel,
        grid=(m // tm, n // tn),
        in_specs=[
            pl.BlockSpec((tm, d), lambda i, j: (i, 0), pipeline_mode=pl.Buffered(1)),
            pl.BlockSpec((1, d), lambda i, j: (0, 0)),
            pl.BlockSpec((d, tn), lambda i, j: (0, j)),
        ],
        out_specs=pl.BlockSpec((tm, tn), lambda i, j: (i, j)),
        out_shape=jax.ShapeDtypeStruct((m, n), F32),
        scratch_shapes=[pltpu.VMEM((tm, d), BF16)],
        compiler_params=_cparams(("arbitrary", "arbitrary")),
        name="norm_inproj",
    )(x2d, g.reshape(1, d), w_bf16)


def _rwkv_kernel(r_ref, k_ref, v_ref, lo_ref,
                 mur_ref, muk_ref, muv_ref, mulo_ref,
                 w0_ref, a0_ref, kk_ref, ka_ref, rk_ref, lng_ref, lnb_ref,
                 w2_ref, a2_ref, g2_ref, mask_ref, seg_ref, tri_ref,
                 y_ref,
                 tail_ref, lotail_ref, loact_ref, s_ref, yraw_ref,
                 *, tt, pairs):
    t = pl.program_id(1)
    g = pl.program_id(2)
    nc = tt // CHUNK
    first = t == 0

    @pl.when(first)
    def _():
        for i in range(pairs * 3):
            tail_ref[g * pairs * 3 + i] = jnp.zeros((SUBLANES, PAIR_W), F32)

    @pl.when(first & (g == 0))
    def _():
        lotail_ref[...] = jnp.zeros_like(lotail_ref)

    @pl.when(g == 0)
    def _():
        lo = lo_ref[0]
        lo_prev = _shift_rows(lo, lotail_ref[...], 1)
        lotail_ref[...] = lo[tt - SUBLANES:]
        xs = lo + (lo_prev - lo) * mulo_ref[...]
        loact_ref[:, 0:LORA_PAD] = jnp.tanh(xs[:, 0:LORA_PAD]).astype(BF16)
        loact_ref[:, LORA_PAD:2 * LORA_PAD] = xs[:, LORA_PAD:2 * LORA_PAD].astype(BF16)
        loact_ref[:, 2 * LORA_PAD:] = jax.nn.sigmoid(xs[:, 2 * LORA_PAD:]).astype(BF16)

    lane = lax.broadcasted_iota(jnp.int32, (1, PAIR_W), 1)
    m0 = lane < HEAD
    strict2 = mask_ref[:, 0:2 * PAIR_W] > 0.5
    incl2 = mask_ref[:, 2 * PAIR_W:4 * PAIR_W] > 0.5
    eye = mask_ref[:, 4 * PAIR_W:]
    tri2 = tri_ref[...]

    def stack(xc):
        return jnp.concatenate([jnp.where(m0, xc, 0.0), jnp.where(m0, 0.0, xc)], axis=0)

    def seg_sum(x):
        return _dot_hilo(x, seg_ref[...])

    def seg_sum_1pass(x):
        return _dot(x.astype(BF16), seg_ref[0:PAIR_W])

    wdt = loact_ref[:, 0:LORA_PAD]
    adx = loact_ref[:, LORA_PAD:2 * LORA_PAD]
    gds = loact_ref[:, 2 * LORA_PAD:]

    def pair_stream(p):
        bodies = []
        ls = slice(p * PAIR_W, (p + 1) * PAIR_W)
        slot = g * pairs + p

        def shifted(ref, mu_ref, idx):
            x = ref[0, :, ls]
            xp = _shift_rows(x, tail_ref[slot * 3 + idx], 1)
            tail_ref[slot * 3 + idx] = x[tt - SUBLANES:]
            return x + (xp - x) * mu_ref[:, ls]

        r = shifted(r_ref, mur_ref, 0)
        k = shifted(k_ref, muk_ref, 1)
        v = shifted(v_ref, muv_ref, 2)

        wlog = -_softplus(-(w0_ref[:, ls] + _dot(wdt, w2_ref[:, ls]))) - 0.5
        lw = -jnp.exp(wlog)
        asig = jax.nn.sigmoid(a0_ref[:, ls] + _dot(adx, a2_ref[:, ls]))
        gate = _dot(gds, g2_ref[:, ls])

        kk = k * kk_ref[:, ls]
        kk = kk * lax.rsqrt(jnp.maximum(seg_sum(kk * kk), 1e-24))
        k = k * (1.0 + (asig - 1.0) * ka_ref[:, ls])
        a_in = -kk
        b_in = kk * asig
        yraw_ref[p, 1] = r * k * rk_ref[:, ls]
        yraw_ref[p, 2] = v
        yraw_ref[p, 3] = gate
        yield

        for c in range(nc):
            cs = slice(c * CHUNK, (c + 1) * CHUNK)
            lwc = lw[cs]
            lwh, lwl = _split(lwc)
            cum = _dot(tri2, jnp.concatenate([lwh, lwl], axis=0))
            cl = cum[CHUNK - 1:CHUNK]
            e_n = jnp.exp(-cum)
            e_l = jnp.exp(cl - cum)
            body = dict(
                at=stack(a_in[cs] * jnp.exp(cum - lwc)).astype(BF16),
                rt=stack(r[cs] * jnp.exp(cum)),
                bk=jnp.concatenate([stack(b_in[cs] * e_n), stack(k[cs] * e_n)], axis=0).astype(BF16),
                bkl=jnp.concatenate([stack(b_in[cs] * e_l), stack(k[cs] * e_l)], axis=0).astype(BF16),
                vs=stack(v[cs]),
                el=jnp.exp(cl),
            )
            bodies.append(body)
            yield

        for bd in bodies:
            sa = jnp.where(strict2, _dot_nt(bd["at"], bd["bk"]), 0.0)
            sr = jnp.where(incl2, _dot_nt(bd["rt"].astype(BF16), bd["bk"]), 0.0)
            bd["n"] = sa[:, :PAIR_W]
            bd["aak"] = sa[:, PAIR_W:].astype(BF16)
            bd["arb"] = sr[:, :PAIR_W].astype(BF16)
            bd["ark"] = sr[:, PAIR_W:].astype(BF16)
            bd["vsb"] = bd["vs"].astype(BF16)
        yield
        for bd in bodies:
            nb = bd["n"].astype(BF16)
            bd["t"] = eye + bd["n"]
            bd["nb"] = _dot(nb, nb).astype(BF16)
        yield
        for _ in range(CHUNK_SHIFT - 2):
            for bd in bodies:
                nb = bd["nb"]
                res = _dot(nb, jnp.concatenate([nb, bd["t"].astype(BF16)], axis=1))
                bd["nb"] = res[:, :PAIR_W].astype(BF16)
                bd["t"] = bd["t"] + res[:, PAIR_W:]
            yield
        for bd in bodies:
            bd["t"] = bd["t"] + _dot(bd["nb"], bd["t"].astype(BF16))
            bd["x0"] = _dot(bd["aak"], bd["vsb"])
        yield
        for bd in bodies:
            bd["wu"] = _dot(bd["t"].astype(BF16), jnp.concatenate([bd["at"], bd["x0"].astype(BF16)], axis=1))
        yield
        for bd in bodies:
            qy = _dot(bd["arb"], bd["wu"].astype(BF16))
            bd["qp"] = (bd["rt"] + qy[:, :PAIR_W]).astype(BF16)
            bd["y0"] = qy[:, PAIR_W:] + _dot(bd["ark"], bd["vsb"])
        yield
        for bd in bodies:
            wut = bd["wu"].T
            bd["mx"] = _dot(wut[:PAIR_W].astype(BF16), bd["bkl"][:PAIR_W]).astype(BF16)
            uv_t = jnp.concatenate([wut[PAIR_W:], bd["vs"].T], axis=1).astype(BF16)
            bd["g0"] = _dot(uv_t, bd["bkl"])
        yield

        s = s_ref[slot]
        for bd in bodies:
            bd["s_in"] = s.astype(BF16)
            s = s * bd["el"] + _dot(bd["s_in"], bd["mx"]) + bd["g0"]
            yield
        s_ref[slot] = s
        for c, bd in enumerate(bodies):
            y2 = _dot_nt(bd["qp"], bd["s_in"]) + bd["y0"]
            yraw_ref[p, 0, c * CHUNK:(c + 1) * CHUNK, :] = y2[:CHUNK] + y2[CHUNK:]
        yield

        y = yraw_ref[p, 0]
        mean = seg_sum_1pass(y) * (1.0 / HEAD)
        yc = y - mean
        var = seg_sum(yc * yc) * (1.0 / HEAD)
        yn = yc * lax.rsqrt(var + GN_EPS) * lng_ref[:, ls] + lnb_ref[:, ls]
        bonus = seg_sum_1pass(yraw_ref[p, 1])
        out = (yn + bonus * yraw_ref[p, 2]) * yraw_ref[p, 3]
        y_ref[0, :, ls] = out.astype(y_ref.dtype)
        yield

    @pl.when(first)
    def _():
        for p in range(pairs):
            s_ref[g * pairs + p] = jnp.zeros((PAIR_W, PAIR_W), F32)

    pending = [pair_stream(p) for p in range(pairs)]
    live = []
    tick = 0
    while pending or live:
        if pending and tick % STREAM_SKEW == 0:
            live.append(pending.pop(0))
        for stream in list(live):
            if next(stream, "done") == "done":
                live.remove(stream)
        tick += 1


def _rwkv_constants():
    idx = np.arange(PAIR_W)
    same = (idx[:, None] // CHUNK) == (idx[None, :] // CHUNK)
    strict = same & (idx[None, :] < idx[:, None])
    incl = same & (idx[None, :] <= idx[:, None])
    masks = np.concatenate([strict, strict, incl, incl, np.eye(PAIR_W, dtype=bool)], axis=1)
    seg2 = np.concatenate([same, same], axis=0)
    tr = np.arange(CHUNK)
    tri = tr[None, :] <= tr[:, None]
    tri2 = np.concatenate([tri, tri], axis=1)
    return (jnp.asarray(masks, F32), jnp.asarray(seg2, BF16), jnp.asarray(tri2, BF16))


def _rwkv_mixer(p3, mu, w0, w2p, a0, a2p, g2, k_k, k_a, r_k, ln_g, ln_b, tt, pairs):
    b, t, _ = p3.shape
    gw = pairs * PAIR_W
    ng = RWKV_W // gw
    row = lambda a: a.reshape(1, -1)
    col_spec = lambda off: pl.BlockSpec((1, gw), lambda bi, ti, gi, off=off: (0, off // gw + gi))
    act_spec = lambda off: pl.BlockSpec((1, tt, gw), lambda bi, ti, gi, off=off: (bi, ti, off // gw + gi))
    const_spec = lambda a: pl.BlockSpec(a.shape, lambda bi, ti, gi: (0, 0))
    kern = functools.partial(_rwkv_kernel, tt=tt, pairs=pairs)
    masks, seg2, tri2 = _rwkv_constants()
    return pl.pallas_call(
        kern,
        grid=(b, t // tt, ng),
        in_specs=[
            act_spec(OFF_R), act_spec(OFF_K), act_spec(OFF_V),
            pl.BlockSpec((1, tt, LORA_W), lambda bi, ti, gi: (bi, ti, OFF_LORA // LORA_W)),
            col_spec(OFF_R), col_spec(OFF_K), col_spec(OFF_V),
            pl.BlockSpec((1, LORA_W), lambda bi, ti, gi: (0, OFF_LORA // LORA_W)),
            col_spec(0), col_spec(0), col_spec(0), col_spec(0), col_spec(0), col_spec(0), col_spec(0),
            pl.BlockSpec((LORA_PAD, gw), lambda bi, ti, gi: (0, gi)),
            pl.BlockSpec((LORA_PAD, gw), lambda bi, ti, gi: (0, gi)),
            pl.BlockSpec((GATE_LORA, gw), lambda bi, ti, gi: (0, gi)),
            const_spec(masks), const_spec(seg2), const_spec(tri2),
        ],
        out_specs=pl.BlockSpec((1, tt, gw), lambda bi, ti, gi: (bi, ti, gi)),
        out_shape=jax.ShapeDtypeStruct((b, t, RWKV_W), BF16),
        scratch_shapes=[
            pltpu.VMEM((ng * pairs * 3, SUBLANES, PAIR_W), F32),
            pltpu.VMEM((SUBLANES, LORA_W), F32),
            pltpu.VMEM((tt, LORA_W), BF16),
            pltpu.VMEM((ng * pairs, PAIR_W, PAIR_W), F32),
            pltpu.VMEM((pairs, 4, tt, PAIR_W), F32),
        ],
        compiler_params=_cparams(("arbitrary", "arbitrary", "arbitrary")),
        name="rwkv7_mixer",
    )(p3, p3, p3, p3, row(mu), row(mu), row(mu), row(mu),
      row(w0), row(a0), row(k_k), row(k_a), row(r_k), row(ln_g), row(ln_b),
      w2p.astype(BF16), a2p.astype(BF16), g2.astype(BF16), masks, seg2, tri2)


def _lru_kernel(x_ref, gt_ref, cw_ref, cb_ref, wr_ref, br_ref, wi_ref, bi_ref, lam_ref, ng_ref,
                y_ref, xtail_ref, h_ref, a_ref, b_ref, *, tt):
    t = pl.program_id(1)
    first = t == 0
    sp = _softplus(-lam_ref[...])
    rowt = lax.broadcasted_iota(jnp.int32, (tt, LRU_BLOCK), 0)
    sub = rowt & (SUBLANES - 1)
    keeps = [(s, sub >= s) for s in (1, 2, 4)]
    is_start = rowt == jnp.where(first, 0, -1)

    @pl.when(first)
    def _():
        xtail_ref[...] = jnp.zeros_like(xtail_ref)
        h_ref[...] = jnp.zeros_like(h_ref)

    for h in range(LRU_BLOCKS):
        ls = slice(h * LRU_BLOCK, (h + 1) * LRU_BLOCK)
        x = x_ref[0, :, ls]
        prev = xtail_ref[:, ls]
        xc = cb_ref[:, ls] + x * cw_ref[CONV_WIDTH - 1:CONV_WIDTH, ls]
        for j in range(CONV_WIDTH - 1):
            xc = xc + _shift_rows(x, prev, CONV_WIDTH - 1 - j) * cw_ref[j:j + 1, ls]
        xtail_ref[:, ls] = x[tt - SUBLANES:]
        xh, xl = _split(xc)
        x2 = jnp.concatenate([xh, xl], axis=1)
        rg = jax.nn.sigmoid(_dot(x2, wr_ref[h]) + br_ref[:, ls])
        ig = jax.nn.sigmoid(_dot(x2, wi_ref[h]) + bi_ref[:, ls])
        log_a = -LRU_C * rg * sp[:, ls]
        a = jnp.exp(log_a)
        th = jnp.tanh(log_a)
        mult = jnp.sqrt(-2.0 * th / (1.0 - th))
        mult = jnp.where(is_start, 1.0, mult)
        bx = mult * ig * xc
        for s, keep in keeps:
            a_s = jnp.where(keep, pltpu.roll(a, s, 0), 1.0)
            b_s = jnp.where(keep, pltpu.roll(bx, s, 0), 0.0)
            bx = a * b_s + bx
            a = a * a_s
        a_ref[:, ls] = a
        b_ref[:, ls] = bx

    def group_step(i, hprev):
        off = pl.multiple_of(i * SUBLANES, SUBLANES)
        hcur = a_ref[pl.ds(off, SUBLANES), :] * hprev + b_ref[pl.ds(off, SUBLANES), :]
        b_ref[pl.ds(off, SUBLANES), :] = hcur
        return jnp.broadcast_to(hcur[SUBLANES - 1:SUBLANES], hcur.shape)

    h_ref[...] = lax.fori_loop(0, tt // SUBLANES, group_step, h_ref[...])

    y = b_ref[...] * jax.nn.gelu(gt_ref[0])
    ms = jnp.mean(y * y, axis=-1, keepdims=True)
    y_ref[0] = (y * lax.rsqrt(ms + NORM_EPS) * ng_ref[...]).astype(y_ref.dtype)


def _lru_mixer(p3, conv_w, conv_b, wr, br, wi, bi, lam, norm_g, tt):
    b, t, _ = p3.shape
    row = lambda a: a.reshape(1, -1)
    full = lambda shape: pl.BlockSpec(shape, lambda bi_, ti: (0,) * len(shape))
    twice = lambda w: jnp.concatenate([w, w], axis=1).astype(BF16)
    return pl.pallas_call(
        functools.partial(_lru_kernel, tt=tt),
        grid=(b, t // tt),
        in_specs=[
            pl.BlockSpec((1, tt, LRU_W), lambda bi_, ti: (bi_, ti, OFF_LRU_X // LRU_W)),
            pl.BlockSpec((1, tt, LRU_W), lambda bi_, ti: (bi_, ti, OFF_LRU_G // LRU_W)),
            full((CONV_WIDTH, LRU_W)), full((1, LRU_W)),
            full((LRU_BLOCKS, 2 * LRU_BLOCK, LRU_BLOCK)), full((1, LRU_W)),
            full((LRU_BLOCKS, 2 * LRU_BLOCK, LRU_BLOCK)), full((1, LRU_W)),
            full((1, LRU_W)), full((1, LRU_W)),
        ],
        out_specs=pl.BlockSpec((1, tt, LRU_W), lambda bi_, ti: (bi_, ti, 0)),
        out_shape=jax.ShapeDtypeStruct((b, t, LRU_W), BF16),
        scratch_shapes=[
            pltpu.VMEM((SUBLANES, LRU_W), F32),
            pltpu.VMEM((SUBLANES, LRU_W), F32),
            pltpu.VMEM((tt, LRU_W), F32),
            pltpu.VMEM((tt, LRU_W), F32),
        ],
        compiler_params=_cparams(("arbitrary", "arbitrary")),
        name="rglru_mixer",
    )(p3, p3, conv_w, row(conv_b), twice(wr), row(br), twice(wi), row(bi), row(lam), row(norm_g))


def _outproj_kernel(ya_ref, yb_ref, wa_ref, wb_ref, x_ref, o_ref):
    o_ref[...] = x_ref[...] + _dot(ya_ref[...], wa_ref[...]) + _dot(yb_ref[...], wb_ref[...])


def _outproj(ya, yb, w_bf16, x2d, tm, tn):
    m, d = x2d.shape
    ka = ya.shape[1]
    kb = yb.shape[1]
    return pl.pallas_call(
        _outproj_kernel,
        grid=(m // tm, d // tn),
        in_specs=[
            pl.BlockSpec((tm, ka), lambda i, j: (i, 0)),
            pl.BlockSpec((tm, kb), lambda i, j: (i, 0)),
            pl.BlockSpec((ka, tn), lambda i, j: (0, j)),
            pl.BlockSpec((kb, tn), lambda i, j: (ka // kb, j)),
            pl.BlockSpec((tm, tn), lambda i, j: (i, j)),
        ],
        out_specs=pl.BlockSpec((tm, tn), lambda i, j: (i, j)),
        out_shape=jax.ShapeDtypeStruct((m, d), F32),
        compiler_params=_cparams(("arbitrary", "arbitrary")),
        name="outproj",
    )(ya, yb, w_bf16, w_bf16, x2d)


def _ffn_up_kernel(h_ref, g_ref, wg_ref, wu_ref, wgt_ref, wut_ref, o_ref, ot_ref, u_ref):
    def swiglu(u, wg, wu):
        return (jax.nn.silu(_dot(u, wg)) * _dot(u, wu)).astype(BF16)

    @pl.when(pl.program_id(1) == 0)
    def _():
        _rmsnorm_rows(h_ref, g_ref, u_ref)
        ot_ref[...] = swiglu(u_ref[...], wgt_ref[...], wut_ref[...])

    o_ref[...] = swiglu(u_ref[...], wg_ref[...], wu_ref[...])


def _split_tail(f, t):
    n_main = f // t
    tail = f - n_main * t
    assert tail > 0 and tail % LANES == 0 and (n_main * t) % tail == 0
    return n_main, tail, n_main * t // tail


def _ffn_up(h2d, g, wg_bf16, wu_bf16, tm, tn):
    m, d = h2d.shape
    f = wg_bf16.shape[1]
    n_main, tail, tail_blk = _split_tail(f, tn)
    once = pl.Buffered(1)
    return pl.pallas_call(
        _ffn_up_kernel,
        grid=(m // tm, n_main),
        in_specs=[
            pl.BlockSpec((tm, d), lambda i, j: (i, 0), pipeline_mode=once),
            pl.BlockSpec((1, d), lambda i, j: (0, 0)),
            pl.BlockSpec((d, tn), lambda i, j: (0, j)),
            pl.BlockSpec((d, tn), lambda i, j: (0, j)),
            pl.BlockSpec((d, tail), lambda i, j: (0, tail_blk), pipeline_mode=once),
            pl.BlockSpec((d, tail), lambda i, j: (0, tail_blk), pipeline_mode=once),
        ],
        out_specs=[pl.BlockSpec((tm, tn), lambda i, j: (i, j)),
                   pl.BlockSpec((tm, tail), lambda i, j: (i, 0))],
        out_shape=[jax.ShapeDtypeStruct((m, n_main * tn), BF16),
                   jax.ShapeDtypeStruct((m, tail), BF16)],
        scratch_shapes=[pltpu.VMEM((tm, d), BF16)],
        compiler_params=_cparams(("arbitrary", "arbitrary")),
        name="ffn_up",
    )(h2d, g.reshape(1, d), wg_bf16, wu_bf16, wg_bf16, wu_bf16)


def _ffn_down_kernel(a_ref, w_ref, at_ref, wt_ref, h_ref, g_ref, o_ref, *, final_norm):
    kk = pl.program_id(1)

    @pl.when(kk == 0)
    def _():
        o_ref[...] = h_ref[...] + _dot(at_ref[...], wt_ref[...])

    o_ref[...] += _dot(a_ref[...], w_ref[...])

    if final_norm:
        @pl.when(kk == pl.num_programs(1) - 1)
        def _():
            _rmsnorm_rows(o_ref, g_ref, o_ref)


def _ffn_down(hid, hid_tail, w_bf16, h2d, final_g, tm, tk):
    m, f_main = hid.shape
    tail = hid_tail.shape[1]
    d = w_bf16.shape[1]
    final_norm = final_g is not None
    g = final_g if final_norm else jnp.ones((d,), F32)
    n_main = f_main // tk
    assert n_main * tk == f_main and f_main % tail == 0 and f_main + tail == w_bf16.shape[0]
    tail_blk = f_main // tail
    return pl.pallas_call(
        functools.partial(_ffn_down_kernel, final_norm=final_norm),
        grid=(m // tm, n_main),
        in_specs=[
            pl.BlockSpec((tm, tk), lambda i, kk: (i, kk)),
            pl.BlockSpec((tk, d), lambda i, kk: (kk, 0)),
            pl.BlockSpec((tm, tail), lambda i, kk: (i, 0)),
            pl.BlockSpec((tail, d), lambda i, kk: (tail_blk, 0)),
            pl.BlockSpec((tm, d), lambda i, kk: (i, 0)),
            pl.BlockSpec((1, d), lambda i, kk: (0, 0)),
        ],
        out_specs=pl.BlockSpec((tm, d), lambda i, kk: (i, 0)),
        out_shape=jax.ShapeDtypeStruct((m, d), F32),
        compiler_params=_cparams(("arbitrary", "arbitrary")),
        name="ffn_down",
    )(hid, w_bf16, hid_tail, w_bf16, h2d, g.reshape(1, d))


def _pad_rows(w, rows):
    return jnp.pad(w, ((0, rows - w.shape[0]), (0, 0)))


def _layer(h2d, bsz, seq, final_g, norm_mix_g, w_in, mu_shift, rwkv_w0, rwkv_w2, rwkv_a0, rwkv_a2,
           rwkv_g2, rwkv_k_k, rwkv_k_a, rwkv_r_k, rwkv_ln_g, rwkv_ln_b, conv_w, conv_b,
           lru_wr, lru_br, lru_wi, lru_bi, lru_lambda, lru_norm_g, w_out,
           norm_ffn_g, ffn_w_gate, ffn_w_up, ffn_w_down):
    d = h2d.shape[1]
    o1, o2 = 3 * RWKV_W, 3 * RWKV_W + DECAY_LORA
    o3 = o2 + AAA_LORA
    o4 = o3 + GATE_LORA

    def relayout(a, fill):
        padw = [(0, 0)] * (a.ndim - 1) + [(0, LORA_PAD - DECAY_LORA)]
        wd = jnp.pad(a[..., o1:o2], padw, constant_values=fill)
        ad = jnp.pad(a[..., o2:o3], padw, constant_values=fill)
        return jnp.concatenate([a[..., o4:], a[..., :o1], wd, ad, a[..., o3:o4]], axis=-1)

    w_in_p = relayout(w_in.astype(BF16), 0.0)
    mu_full = jnp.concatenate([mu_shift, jnp.zeros((2 * LRU_W,), F32)])
    mu_p = relayout(mu_full, 0.0)
    w2p = _pad_rows(rwkv_w2, LORA_PAD)
    a2p = _pad_rows(rwkv_a2, LORA_PAD)

    p = _norm_matmul(h2d, norm_mix_g, w_in_p, tm=1024, tn=512)
    p3 = p.reshape(bsz, seq, P_COLS)
    y_a = _rwkv_mixer(p3, mu_p, rwkv_w0, w2p, rwkv_a0, a2p, rwkv_g2, rwkv_k_k, rwkv_k_a,
                      rwkv_r_k.reshape(-1), rwkv_ln_g, rwkv_ln_b, tt=512, pairs=4)
    y_b = _lru_mixer(p3, conv_w, conv_b, lru_wr, lru_br, lru_wi, lru_bi, lru_lambda, lru_norm_g, tt=256)
    h1 = _outproj(y_a.reshape(-1, RWKV_W), y_b.reshape(-1, LRU_W), w_out.astype(BF16), h2d, tm=1024, tn=1024)
    hid, hid_tail = _ffn_up(h1, norm_ffn_g, ffn_w_gate.astype(BF16), ffn_w_up.astype(BF16), tm=1024, tn=384)
    return _ffn_down(hid, hid_tail, ffn_w_down.astype(BF16), h1, final_g, tm=512, tk=768)


def kernel(x, norm_mix_g, w_in, mu_shift, rwkv_w0, rwkv_w2, rwkv_a0, rwkv_a2, rwkv_g2, rwkv_k_k, rwkv_k_a, rwkv_r_k, rwkv_ln_g, rwkv_ln_b, conv_w, conv_b, lru_wr, lru_br, lru_wi, lru_bi, lru_lambda, lru_norm_g, w_out, norm_ffn_g, ffn_w_gate, ffn_w_up, ffn_w_down, norm_final_g):
    bsz, seq, d = x.shape
    h = x.astype(F32).reshape(bsz * seq, d)
    depth = w_in.shape[0]
    for l in range(depth):
        last = l == depth - 1
        h = _layer(h, bsz, seq, norm_final_g if last else None, norm_mix_g[l], w_in[l], mu_shift[l], rwkv_w0[l], rwkv_w2[l], rwkv_a0[l],
                   rwkv_a2[l], rwkv_g2[l], rwkv_k_k[l], rwkv_k_a[l], rwkv_r_k[l], rwkv_ln_g[l],
                   rwkv_ln_b[l], conv_w[l], conv_b[l], lru_wr[l], lru_br[l], lru_wi[l], lru_bi[l],
                   lru_lambda[l], lru_norm_g[l], w_out[l], norm_ffn_g[l], ffn_w_gate[l], ffn_w_up[l],
                   ffn_w_down[l])
    return h.reshape(bsz, seq, d).astype(x.dtype)
```

```python
import functools

import jax
import jax.numpy as jnp
import numpy as np
from jax import lax
from jax.experimental import pallas as pl
from jax.experimental.pallas import tpu as pltpu

F32 = jnp.float32
BF16 = jnp.bfloat16

D_MODEL = 4096
RWKV_W = 2048
HEAD = 64
DECAY_LORA = 96
AAA_LORA = 96
GATE_LORA = 256
LRU_W = 2048
LRU_BLOCK = 128
LRU_BLOCKS = LRU_W // LRU_BLOCK
CONV_WIDTH = 4
LRU_C = 8.0
NORM_EPS = 1e-6
GN_EPS = 64e-5

LANES = 128
SUBLANES = 8
LORA_PAD = 128
LORA_W = 2 * LORA_PAD + GATE_LORA
OFF_LRU_X = 0
OFF_LRU_G = LRU_W
OFF_R = 2 * LRU_W
OFF_K = OFF_R + RWKV_W
OFF_V = OFF_K + RWKV_W
OFF_LORA = OFF_V + RWKV_W
P_COLS = OFF_LORA + LORA_W

VMEM_LIMIT = 56 * 1024 * 1024

CHUNK = 64
CHUNK_SHIFT = CHUNK.bit_length() - 1
PAIR_W = 2 * HEAD
STREAM_SKEW = 6


def _cparams(sem):
    return pltpu.CompilerParams(dimension_semantics=sem, vmem_limit_bytes=VMEM_LIMIT)


def _dot(a, b):
    return jnp.dot(a, b, preferred_element_type=F32)


def _dot_nt(a, b):
    return lax.dot_general(a, b, (((1,), (1,)), ((), ())), preferred_element_type=F32)


def _split(x):
    hi = x.astype(BF16)
    lo = (x - hi.astype(F32)).astype(BF16)
    return hi, lo


def _dot_hilo(a, b2):
    ah, al = _split(a)
    return _dot(jnp.concatenate([ah, al], axis=1), b2)


def _softplus(x):
    return jnp.maximum(x, 0.0) + jnp.log1p(jnp.exp(-jnp.abs(x)))


def _shift_rows(x, prev_tail, s):
    rolled = pltpu.roll(x, s, 0)
    fix = pltpu.roll(prev_tail, s, 0)
    row = lax.broadcasted_iota(jnp.int32, (SUBLANES, x.shape[1]), 0)
    head = jnp.where(row < s, fix, rolled[:SUBLANES])
    return jnp.concatenate([head, rolled[SUBLANES:]], axis=0)


def _norm_matmul_kernel(x_ref, g_ref, w_ref, o_ref, u_ref):
    @pl.when(pl.program_id(1) == 0)
    def _():
        x = x_ref[...]
        ms = jnp.mean(x * x, axis=-1, keepdims=True)
        u_ref[...] = (x * lax.rsqrt(ms + NORM_EPS) * g_ref[...]).astype(BF16)

    o_ref[...] = _dot(u_ref[...], w_ref[...]).astype(o_ref.dtype)


def _norm_matmul(x2d, g, w_bf16, tm, tn):
    m, d = x2d.shape
    n = w_bf16.shape[1]
    return pl.pallas_call(
        _norm_matmul_kernel,
        grid=(m // tm, n // tn),
        in_specs=[
            pl.BlockSpec((tm, d), lambda i, j: (i, 0), pipeline_mode=pl.Buffered(1)),
            pl.BlockSpec((1, d), lambda i, j: (0, 0)),
            pl.BlockSpec((d, tn), lambda i, j: (0, j)),
        ],
        out_specs=pl.BlockSpec((tm, tn), lambda i, j: (i, j)),
        out_shape=jax.ShapeDtypeStruct((m, n), F32),
        scratch_shapes=[pltpu.VMEM((tm, d), BF16)],
        compiler_params=_cparams(("arbitrary", "arbitrary")),
        name="norm_inproj",
    )(x2d, g.reshape(1, d), w_bf16)


def _rwkv_kernel(r_ref, k_ref, v_ref, lo_ref,
                 mur_ref, muk_ref, muv_ref, mulo_ref,
                 w0_ref, a0_ref, kk_ref, ka_ref, rk_ref, lng_ref, lnb_ref,
                 w2_ref, a2_ref, g2_ref, mask_ref, seg_ref, tri_ref,
                 y_ref,
                 tail_ref, lotail_ref, loact_ref, s_ref, yraw_ref,
                 *, tt, pairs):
    t = pl.program_id(1)
    g = pl.program_id(2)
    nc = tt // CHUNK
    first = t == 0

    @pl.when(first)
    def _():
        for i in range(pairs * 3):
            tail_ref[g * pairs * 3 + i] = jnp.zeros((SUBLANES, PAIR_W), F32)

    @pl.when(first & (g == 0))
    def _():
        lotail_ref[...] = jnp.zeros_like(lotail_ref)

    @pl.when(g == 0)
    def _():
        lo = lo_ref[0]
        lo_prev = _shift_rows(lo, lotail_ref[...], 1)
        lotail_ref[...] = lo[tt - SUBLANES:]
        xs = lo + (lo_prev - lo) * mulo_ref[...]
        loact_ref[:, 0:LORA_PAD] = jnp.tanh(xs[:, 0:LORA_PAD]).astype(BF16)
        loact_ref[:, LORA_PAD:2 * LORA_PAD] = xs[:, LORA_PAD:2 * LORA_PAD].astype(BF16)
        loact_ref[:, 2 * LORA_PAD:] = jax.nn.sigmoid(xs[:, 2 * LORA_PAD:]).astype(BF16)

    lane = lax.broadcasted_iota(jnp.int32, (1, PAIR_W), 1)
    m0 = lane < HEAD
    strict2 = mask_ref[:, 0:2 * PAIR_W] > 0.5
    incl2 = mask_ref[:, 2 * PAIR_W:4 * PAIR_W] > 0.5
    eye = mask_ref[:, 4 * PAIR_W:]
    tri2 = tri_ref[...]

    def stack(xc):
        return jnp.concatenate([jnp.where(m0, xc, 0.0), jnp.where(m0, 0.0, xc)], axis=0)

    def seg_sum(x):
        return _dot_hilo(x, seg_ref[...])

    def seg_sum_1pass(x):
        return _dot(x.astype(BF16), seg_ref[0:PAIR_W])

    wdt = loact_ref[:, 0:LORA_PAD]
    adx = loact_ref[:, LORA_PAD:2 * LORA_PAD]
    gds = loact_ref[:, 2 * LORA_PAD:]

    def pair_stream(p):
        bodies = []
        ls = slice(p * PAIR_W, (p + 1) * PAIR_W)
        slot = g * pairs + p

        def shifted(ref, mu_ref, idx):
            x = ref[0, :, ls]
            xp = _shift_rows(x, tail_ref[slot * 3 + idx], 1)
            tail_ref[slot * 3 + idx] = x[tt - SUBLANES:]
            return x + (xp - x) * mu_ref[:, ls]

        r = shifted(r_ref, mur_ref, 0)
        k = shifted(k_ref, muk_ref, 1)
        v = shifted(v_ref, muv_ref, 2)

        wlog = -_softplus(-(w0_ref[:, ls] + _dot(wdt, w2_ref[:, ls]))) - 0.5
        lw = -jnp.exp(wlog)
        asig = jax.nn.sigmoid(a0_ref[:, ls] + _dot(adx, a2_ref[:, ls]))
        gate = _dot(gds, g2_ref[:, ls])

        kk = k * kk_ref[:, ls]
        kk = kk * lax.rsqrt(jnp.maximum(seg_sum(kk * kk), 1e-24))
        k = k * (1.0 + (asig - 1.0) * ka_ref[:, ls])
        a_in = -kk
        b_in = kk * asig
        yraw_ref[p, 1] = r * k * rk_ref[:, ls]
        yraw_ref[p, 2] = v
        yraw_ref[p, 3] = gate
        yield

        for c in range(nc):
            cs = slice(c * CHUNK, (c + 1) * CHUNK)
            lwc = lw[cs]
            lwh, lwl = _split(lwc)
            cum = _dot(tri2, jnp.concatenate([lwh, lwl], axis=0))
            cl = cum[CHUNK - 1:CHUNK]
            e_n = jnp.exp(-cum)
            e_l = jnp.exp(cl - cum)
            body = dict(
                at=stack(a_in[cs] * jnp.exp(cum - lwc)).astype(BF16),
                rt=stack(r[cs] * jnp.exp(cum)),
                bk=jnp.concatenate([stack(b_in[cs] * e_n), stack(k[cs] * e_n)], axis=0).astype(BF16),
                bkl=jnp.concatenate([stack(b_in[cs] * e_l), stack(k[cs] * e_l)], axis=0).astype(BF16),
                vs=stack(v[cs]),
                el=jnp.exp(cl),
            )
            bodies.append(body)
            yield

        for bd in bodies:
            sa = jnp.where(strict2, _dot_nt(bd["at"], bd["bk"]), 0.0)
            sr = jnp.where(incl2, _dot_nt(bd["rt"].astype(BF16), bd["bk"]), 0.0)
            bd["n"] = sa[:, :PAIR_W]
            bd["aak"] = sa[:, PAIR_W:].astype(BF16)
            bd["arb"] = sr[:, :PAIR_W].astype(BF16)
            bd["ark"] = sr[:, PAIR_W:].astype(BF16)
            bd["vsb"] = bd["vs"].astype(BF16)
        yield
        for bd in bodies:
            nb = bd["n"].astype(BF16)
            bd["t"] = eye + bd["n"]
            bd["nb"] = _dot(nb, nb).astype(BF16)
        yield
        for _ in range(CHUNK_SHIFT - 2):
            for bd in bodies:
                nb = bd["nb"]
                res = _dot(nb, jnp.concatenate([nb, bd["t"].astype(BF16)], axis=1))
                bd["nb"] = res[:, :PAIR_W].astype(BF16)
                bd["t"] = bd["t"] + res[:, PAIR_W:]
            yield
        for bd in bodies:
            bd["t"] = bd["t"] + _dot(bd["nb"], bd["t"].astype(BF16))
            bd["x0"] = _dot(bd["aak"], bd["vsb"])
        yield
        for bd in bodies:
            bd["wu"] = _dot(bd["t"].astype(BF16), jnp.concatenate([bd["at"], bd["x0"].astype(BF16)], axis=1))
        yield
        for bd in bodies:
            qy = _dot(bd["arb"], bd["wu"].astype(BF16))
            bd["qp"] = (bd["rt"] + qy[:, :PAIR_W]).astype(BF16)
            bd["y0"] = qy[:, PAIR_W:] + _dot(bd["ark"], bd["vsb"])
        yield
        for bd in bodies:
            wut = bd["wu"].T
            bd["mx"] = _dot(wut[:PAIR_W].astype(BF16), bd["bkl"][:PAIR_W]).astype(BF16)
            uv_t = jnp.concatenate([wut[PAIR_W:], bd["vs"].T], axis=1).astype(BF16)
            bd["g0"] = _dot(uv_t, bd["bkl"])
        yield

        s = s_ref[slot]
        for bd in bodies:
            bd["s_in"] = s.astype(BF16)
            s = s * bd["el"] + _dot(bd["s_in"], bd["mx"]) + bd["g0"]
            yield
        s_ref[slot] = s
        for c, bd in enumerate(bodies):
            y2 = _dot_nt(bd["qp"], bd["s_in"]) + bd["y0"]
            yraw_ref[p, 0, c * CHUNK:(c + 1) * CHUNK, :] = y2[:CHUNK] + y2[CHUNK:]
        yield

        y = yraw_ref[p, 0]
        mean = seg_sum_1pass(y) * (1.0 / HEAD)
        yc = y - mean
        var = seg_sum(yc * yc) * (1.0 / HEAD)
        yn = yc * lax.rsqrt(var + GN_EPS) * lng_ref[:, ls] + lnb_ref[:, ls]
        bonus = seg_sum_1pass(yraw_ref[p, 1])
        out = (yn + bonus * yraw_ref[p, 2]) * yraw_ref[p, 3]
        y_ref[0, :, ls] = out.astype(y_ref.dtype)
        yield

    @pl.when(first)
    def _():
        for p in range(pairs):
            s_ref[g * pairs + p] = jnp.zeros((PAIR_W, PAIR_W), F32)

    pending = [pair_stream(p) for p in range(pairs)]
    live = []
    tick = 0
    while pending or live:
        if pending and tick % STREAM_SKEW == 0:
            live.append(pending.pop(0))
        for stream in list(live):
            if next(stream, "done") == "done":
                live.remove(stream)
        tick += 1


def _rwkv_constants():
    idx = np.arange(PAIR_W)
    same = (idx[:, None] // CHUNK) == (idx[None, :] // CHUNK)
    strict = same & (idx[None, :] < idx[:, None])
    incl = same & (idx[None, :] <= idx[:, None])
    masks = np.concatenate([strict, strict, incl, incl, np.eye(PAIR_W, dtype=bool)], axis=1)
    seg2 = np.concatenate([same, same], axis=0)
    tr = np.arange(CHUNK)
    tri = tr[None, :] <= tr[:, None]
    tri2 = np.concatenate([tri, tri], axis=1)
    return (jnp.asarray(masks, F32), jnp.asarray(seg2, BF16), jnp.asarray(tri2, BF16))


def _rwkv_mixer(p3, mu, w0, w2p, a0, a2p, g2, k_k, k_a, r_k, ln_g, ln_b, tt, pairs):
    b, t, _ = p3.shape
    gw = pairs * PAIR_W
    ng = RWKV_W // gw
    row = lambda a: a.reshape(1, -1)
    col_spec = lambda off: pl.BlockSpec((1, gw), lambda bi, ti, gi, off=off: (0, off // gw + gi))
    act_spec = lambda off: pl.BlockSpec((1, tt, gw), lambda bi, ti, gi, off=off: (bi, ti, off // gw + gi))
    const_spec = lambda a: pl.BlockSpec(a.shape, lambda bi, ti, gi: (0, 0))
    kern = functools.partial(_rwkv_kernel, tt=tt, pairs=pairs)
    masks, seg2, tri2 = _rwkv_constants()
    return pl.pallas_call(
        kern,
        grid=(b, t // tt, ng),
        in_specs=[
            act_spec(OFF_R), act_spec(OFF_K), act_spec(OFF_V),
            pl.BlockSpec((1, tt, LORA_W), lambda bi, ti, gi: (bi, ti, OFF_LORA // LORA_W)),
            col_spec(OFF_R), col_spec(OFF_K), col_spec(OFF_V),
            pl.BlockSpec((1, LORA_W), lambda bi, ti, gi: (0, OFF_LORA // LORA_W)),
            col_spec(0), col_spec(0), col_spec(0), col_spec(0), col_spec(0), col_spec(0), col_spec(0),
            pl.BlockSpec((LORA_PAD, gw), lambda bi, ti, gi: (0, gi)),
            pl.BlockSpec((LORA_PAD, gw), lambda bi, ti, gi: (0, gi)),
            pl.BlockSpec((GATE_LORA, gw), lambda bi, ti, gi: (0, gi)),
            const_spec(masks), const_spec(seg2), const_spec(tri2),
        ],
        out_specs=pl.BlockSpec((1, tt, gw), lambda bi, ti, gi: (bi, ti, gi)),
        out_shape=jax.ShapeDtypeStruct((b, t, RWKV_W), BF16),
        scratch_shapes=[
            pltpu.VMEM((ng * pairs * 3, SUBLANES, PAIR_W), F32),
            pltpu.VMEM((SUBLANES, LORA_W), F32),
            pltpu.VMEM((tt, LORA_W), BF16),
            pltpu.VMEM((ng * pairs, PAIR_W, PAIR_W), F32),
            pltpu.VMEM((pairs, 4, tt, PAIR_W), F32),
        ],
        compiler_params=_cparams(("arbitrary", "arbitrary", "arbitrary")),
        name="rwkv7_mixer",
    )(p3, p3, p3, p3, row(mu), row(mu), row(mu), row(mu),
      row(w0), row(a0), row(k_k), row(k_a), row(r_k), row(ln_g), row(ln_b),
      w2p.astype(BF16), a2p.astype(BF16), g2.astype(BF16), masks, seg2, tri2)


def _lru_kernel(x_ref, gt_ref, cw_ref, cb_ref, wr_ref, br_ref, wi_ref, bi_ref, lam_ref, ng_ref,
                y_ref, xtail_ref, h_ref, a_ref, b_ref, *, tt):
    t = pl.program_id(1)
    first = t == 0
    sp = _softplus(-lam_ref[...])
    rowt = lax.broadcasted_iota(jnp.int32, (tt, LRU_BLOCK), 0)
    sub = rowt & (SUBLANES - 1)
    keeps = [(s, sub >= s) for s in (1, 2, 4)]
    is_start = rowt == jnp.where(first, 0, -1)

    @pl.when(first)
    def _():
        xtail_ref[...] = jnp.zeros_like(xtail_ref)
        h_ref[...] = jnp.zeros_like(h_ref)

    for h in range(LRU_BLOCKS):
        ls = slice(h * LRU_BLOCK, (h + 1) * LRU_BLOCK)
        x = x_ref[0, :, ls]
        prev = xtail_ref[:, ls]
        xc = cb_ref[:, ls] + x * cw_ref[CONV_WIDTH - 1:CONV_WIDTH, ls]
        for j in range(CONV_WIDTH - 1):
            xc = xc + _shift_rows(x, prev, CONV_WIDTH - 1 - j) * cw_ref[j:j + 1, ls]
        xtail_ref[:, ls] = x[tt - SUBLANES:]
        xh, xl = _split(xc)
        x2 = jnp.concatenate([xh, xl], axis=1)
        rg = jax.nn.sigmoid(_dot(x2, wr_ref[h]) + br_ref[:, ls])
        ig = jax.nn.sigmoid(_dot(x2, wi_ref[h]) + bi_ref[:, ls])
        log_a = -LRU_C * rg * sp[:, ls]
        a = jnp.exp(log_a)
        th = jnp.tanh(log_a)
        mult = jnp.sqrt(-2.0 * th / (1.0 - th))
        mult = jnp.where(is_start, 1.0, mult)
        bx = mult * ig * xc
        for s, keep in keeps:
            a_s = jnp.where(keep, pltpu.roll(a, s, 0), 1.0)
            b_s = jnp.where(keep, pltpu.roll(bx, s, 0), 0.0)
            bx = a * b_s + bx
            a = a * a_s
        a_ref[:, ls] = a
        b_ref[:, ls] = bx

    def group_step(i, hprev):
        off = pl.multiple_of(i * SUBLANES, SUBLANES)
        hcur = a_ref[pl.ds(off, SUBLANES), :] * hprev + b_ref[pl.ds(off, SUBLANES), :]
        b_ref[pl.ds(off, SUBLANES), :] = hcur
        return jnp.broadcast_to(hcur[SUBLANES - 1:SUBLANES], hcur.shape)

    h_ref[...] = lax.fori_loop(0, tt // SUBLANES, group_step, h_ref[...])

    y = b_ref[...] * jax.nn.gelu(gt_ref[0])
    ms = jnp.mean(y * y, axis=-1, keepdims=True)
    y_ref[0] = (y * lax.rsqrt(ms + NORM_EPS) * ng_ref[...]).astype(y_ref.dtype)


def _lru_mixer(p3, conv_w, conv_b, wr, br, wi, bi, lam, norm_g, tt):
    b, t, _ = p3.shape
    row = lambda a: a.reshape(1, -1)
    full = lambda shape: pl.BlockSpec(shape, lambda bi_, ti: (0,) * len(shape))
    twice = lambda w: jnp.concatenate([w, w], axis=1).astype(BF16)
    return pl.pallas_call(
        functools.partial(_lru_kernel, tt=tt),
        grid=(b, t // tt),
        in_specs=[
            pl.BlockSpec((1, tt, LRU_W), lambda bi_, ti: (bi_, ti, OFF_LRU_X // LRU_W)),
            pl.BlockSpec((1, tt, LRU_W), lambda bi_, ti: (bi_, ti, OFF_LRU_G // LRU_W)),
            full((CONV_WIDTH, LRU_W)), full((1, LRU_W)),
            full((LRU_BLOCKS, 2 * LRU_BLOCK, LRU_BLOCK)), full((1, LRU_W)),
            full((LRU_BLOCKS, 2 * LRU_BLOCK, LRU_BLOCK)), full((1, LRU_W)),
            full((1, LRU_W)), full((1, LRU_W)),
        ],
        out_specs=pl.BlockSpec((1, tt, LRU_W), lambda bi_, ti: (bi_, ti, 0)),
        out_shape=jax.ShapeDtypeStruct((b, t, LRU_W), BF16),
        scratch_shapes=[
            pltpu.VMEM((SUBLANES, LRU_W), F32),
            pltpu.VMEM((SUBLANES, LRU_W), F32),
            pltpu.VMEM((tt, LRU_W), F32),
            pltpu.VMEM((tt, LRU_W), F32),
        ],
        compiler_params=_cparams(("arbitrary", "arbitrary")),
        name="rglru_mixer",
    )(p3, p3, conv_w, row(conv_b), twice(wr), row(br), twice(wi), row(bi), row(lam), row(norm_g))


def _outproj_kernel(ya_ref, yb_ref, wa_ref, wb_ref, x_ref, o_ref):
    o_ref[...] = x_ref[...] + _dot(ya_ref[...], wa_ref[...]) + _dot(yb_ref[...], wb_ref[...])


def _outproj(ya, yb, w_bf16, x2d, tm, tn):
    m, d = x2d.shape
    ka = ya.shape[1]
    kb = yb.shape[1]
    return pl.pallas_call(
        _outproj_kernel,
        grid=(m // tm, d // tn),
        in_specs=[
            pl.BlockSpec((tm, ka), lambda i, j: (i, 0)),
            pl.BlockSpec((tm, kb), lambda i, j: (i, 0)),
            pl.BlockSpec((ka, tn), lambda i, j: (0, j)),
            pl.BlockSpec((kb, tn), lambda i, j: (ka // kb, j)),
            pl.BlockSpec((tm, tn), lambda i, j: (i, j)),
        ],
        out_specs=pl.BlockSpec((tm, tn), lambda i, j: (i, j)),
        out_shape=jax.ShapeDtypeStruct((m, d), F32),
        compiler_params=_cparams(("arbitrary", "arbitrary")),
        name="outproj",
    )(ya, yb, w_bf16, w_bf16, x2d)


def _ffn_up_kernel(h_ref, g_ref, wg_ref, wu_ref, o_ref, u_ref):
    @pl.when(pl.program_id(1) == 0)
    def _():
        x = h_ref[...]
        ms = jnp.mean(x * x, axis=-1, keepdims=True)
        u_ref[...] = (x * lax.rsqrt(ms + NORM_EPS) * g_ref[...]).astype(BF16)

    u = u_ref[...]
    gate = _dot(u, wg_ref[...])
    up = _dot(u, wu_ref[...])
    o_ref[...] = (jax.nn.silu(gate) * up).astype(o_ref.dtype)


def _ffn_up(h2d, g, wg_bf16, wu_bf16, tm, tn):
    m, d = h2d.shape
    f = wg_bf16.shape[1]
    return pl.pallas_call(
        _ffn_up_kernel,
        grid=(m // tm, f // tn),
        in_specs=[
            pl.BlockSpec((tm, d), lambda i, j: (i, 0), pipeline_mode=pl.Buffered(1)),
            pl.BlockSpec((1, d), lambda i, j: (0, 0)),
            pl.BlockSpec((d, tn), lambda i, j: (0, j)),
            pl.BlockSpec((d, tn), lambda i, j: (0, j)),
        ],
        out_specs=pl.BlockSpec((tm, tn), lambda i, j: (i, j)),
        out_shape=jax.ShapeDtypeStruct((m, f), BF16),
        scratch_shapes=[pltpu.VMEM((tm, d), BF16)],
        compiler_params=_cparams(("arbitrary", "arbitrary")),
        name="ffn_up",
    )(h2d, g.reshape(1, d), wg_bf16, wu_bf16)


def _ffn_down_kernel(a_ref, w_ref, at_ref, wt_ref, h_ref, g_ref, o_ref, *, final_norm):
    kk = pl.program_id(1)

    @pl.when(kk == 0)
    def _():
        o_ref[...] = h_ref[...] + _dot(at_ref[...], wt_ref[...])

    o_ref[...] += _dot(a_ref[...], w_ref[...])

    if final_norm:
        @pl.when(kk == pl.num_programs(1) - 1)
        def _():
            x = o_ref[...]
            ms = jnp.mean(x * x, axis=-1, keepdims=True)
            o_ref[...] = x * lax.rsqrt(ms + NORM_EPS) * g_ref[...]


def _ffn_down(hid, w_bf16, h2d, final_g, tm, tk):
    m, f = hid.shape
    d = w_bf16.shape[1]
    final_norm = final_g is not None
    g = final_g if final_norm else jnp.ones((d,), F32)
    n_main = f // tk
    tail = f - n_main * tk
    assert tail > 0 and tail % LANES == 0 and (n_main * tk) % tail == 0
    tail_blk = n_main * tk // tail
    return pl.pallas_call(
        functools.partial(_ffn_down_kernel, final_norm=final_norm),
        grid=(m // tm, n_main),
        in_specs=[
            pl.BlockSpec((tm, tk), lambda i, kk: (i, kk)),
            pl.BlockSpec((tk, d), lambda i, kk: (kk, 0)),
            pl.BlockSpec((tm, tail), lambda i, kk: (i, tail_blk)),
            pl.BlockSpec((tail, d), lambda i, kk: (tail_blk, 0)),
            pl.BlockSpec((tm, d), lambda i, kk: (i, 0)),
            pl.BlockSpec((1, d), lambda i, kk: (0, 0)),
        ],
        out_specs=pl.BlockSpec((tm, d), lambda i, kk: (i, 0)),
        out_shape=jax.ShapeDtypeStruct((m, d), F32),
        compiler_params=_cparams(("arbitrary", "arbitrary")),
        name="ffn_down",
    )(hid, w_bf16, hid, w_bf16, h2d, g.reshape(1, d))


def _pad_rows(w, rows):
    return jnp.pad(w, ((0, rows - w.shape[0]), (0, 0)))


def _layer(h2d, bsz, seq, final_g, norm_mix_g, w_in, mu_shift, rwkv_w0, rwkv_w2, rwkv_a0, rwkv_a2,
           rwkv_g2, rwkv_k_k, rwkv_k_a, rwkv_r_k, rwkv_ln_g, rwkv_ln_b, conv_w, conv_b,
           lru_wr, lru_br, lru_wi, lru_bi, lru_lambda, lru_norm_g, w_out,
           norm_ffn_g, ffn_w_gate, ffn_w_up, ffn_w_down):
    d = h2d.shape[1]
    o1, o2 = 3 * RWKV_W, 3 * RWKV_W + DECAY_LORA
    o3 = o2 + AAA_LORA
    o4 = o3 + GATE_LORA

    def relayout(a, fill):
        padw = [(0, 0)] * (a.ndim - 1) + [(0, LORA_PAD - DECAY_LORA)]
        wd = jnp.pad(a[..., o1:o2], padw, constant_values=fill)
        ad = jnp.pad(a[..., o2:o3], padw, constant_values=fill)
        return jnp.concatenate([a[..., o4:], a[..., :o1], wd, ad, a[..., o3:o4]], axis=-1)

    w_in_p = relayout(w_in.astype(BF16), 0.0)
    mu_full = jnp.concatenate([mu_shift, jnp.zeros((2 * LRU_W,), F32)])
    mu_p = relayout(mu_full, 0.0)
    w2p = _pad_rows(rwkv_w2, LORA_PAD)
    a2p = _pad_rows(rwkv_a2, LORA_PAD)

    p = _norm_matmul(h2d, norm_mix_g, w_in_p, tm=1024, tn=512)
    p3 = p.reshape(bsz, seq, P_COLS)
    y_a = _rwkv_mixer(p3, mu_p, rwkv_w0, w2p, rwkv_a0, a2p, rwkv_g2, rwkv_k_k, rwkv_k_a,
                      rwkv_r_k.reshape(-1), rwkv_ln_g, rwkv_ln_b, tt=512, pairs=8)
    y_b = _lru_mixer(p3, conv_w, conv_b, lru_wr, lru_br, lru_wi, lru_bi, lru_lambda, lru_norm_g, tt=256)
    h1 = _outproj(y_a.reshape(-1, RWKV_W), y_b.reshape(-1, LRU_W), w_out.astype(BF16), h2d, tm=1024, tn=1024)
    hid = _ffn_up(h1, norm_ffn_g, ffn_w_gate.astype(BF16), ffn_w_up.astype(BF16), tm=1024, tn=256)
    return _ffn_down(hid, ffn_w_down.astype(BF16), h1, final_g, tm=512, tk=768)


def kernel(x, norm_mix_g, w_in, mu_shift, rwkv_w0, rwkv_w2, rwkv_a0, rwkv_a2, rwkv_g2, rwkv_k_k, rwkv_k_a, rwkv_r_k, rwkv_ln_g, rwkv_ln_b, conv_w, conv_b, lru_wr, lru_br, lru_wi, lru_bi, lru_lambda, lru_norm_g, w_out, norm_ffn_g, ffn_w_gate, ffn_w_up, ffn_w_down, norm_final_g):
    bsz, seq, d = x.shape
    h = x.astype(F32).reshape(bsz * seq, d)
    depth = w_in.shape[0]
    for l in range(depth):
        last = l == depth - 1
        h = _layer(h, bsz, seq, norm_final_g if last else None, norm_mix_g[l], w_in[l], mu_shift[l], rwkv_w0[l], rwkv_w2[l], rwkv_a0[l],
                   rwkv_a2[l], rwkv_g2[l], rwkv_k_k[l], rwkv_k_a[l], rwkv_r_k[l], rwkv_ln_g[l],
                   rwkv_ln_b[l], conv_w[l], conv_b[l], lru_wr[l], lru_br[l], lru_wi[l], lru_bi[l],
                   lru_lambda[l], lru_norm_g[l], w_out[l], norm_ffn_g[l], ffn_w_gate[l], ffn_w_up[l],
                   ffn_w_down[l])
    return h.reshape(bsz, seq, d).astype(x.dtype)
```
